```python
import jax, jax.numpy as jnp
from jax import lax
import numpy as np

D_MODEL = 1024
BATCH = 2
SEQ = 8192
DEPTH = 1

N_META = 16
BLOCK = 128
PAD = BLOCK - N_META
MLA_HEADS = 8
MLA_NOPE = 64
MLA_ROPE = 32
MLA_V = 64
Q_LORA = 512
KV_LORA = 256
ROPE_THETA = 10000.0
FOX_HEADS = 8
FOX_HD = 64
N_GROUPS = 4
EXPERTS_PER_GROUP = 8
TOP_K = 2
D_EXPERT = 256
EPS = 1e-6
NEG = -1e30
IN_SPLITS = (Q_LORA, KV_LORA, MLA_ROPE, FOX_HEADS * FOX_HD, FOX_HEADS * FOX_HD, FOX_HEADS * FOX_HD, FOX_HEADS, D_MODEL, D_MODEL)
D_IN = sum(IN_SPLITS)

kernel_name = "hybrid_mla_fox_hmoe_block"


def rms_norm(x, g):
    xf = x.astype(jnp.float32)
    y = xf * lax.rsqrt(jnp.mean(xf * xf, axis=-1, keepdims=True) + EPS)
    return (y * g.astype(jnp.float32)).astype(x.dtype)


def rope_tables(length):
    pos = jnp.arange(length, dtype=jnp.float32)
    inv = ROPE_THETA ** (-jnp.arange(0, MLA_ROPE, 2, dtype=jnp.float32) / MLA_ROPE)
    ang = pos[:, None] * inv[None, :]
    return jnp.cos(ang), jnp.sin(ang)


def apply_rope(x, cos, sin):
    xf = x.astype(jnp.float32)
    x1, x2 = xf[..., : MLA_ROPE // 2], xf[..., MLA_ROPE // 2:]
    c, s = cos[None, :, None, :], sin[None, :, None, :]
    return jnp.concatenate([x1 * c - x2 * s, x1 * s + x2 * c], axis=-1).astype(x.dtype)


def blocked_causal_attention(q, k, v, decay, scale):
    B, L, H, _ = q.shape
    dv = v.shape[-1]
    lp = L + PAD
    nb = lp // BLOCK

    def prep(t):
        return jnp.pad(t, ((0, 0), (PAD, 0)) + ((0, 0),) * (t.ndim - 2)).swapaxes(1, 2)

    qp = prep(q.astype(jnp.float32)) * scale
    kp = prep(k.astype(jnp.float32))
    vp = prep(v)
    q_blocks = qp.reshape(B, H, nb, BLOCK, -1).transpose(2, 0, 1, 3, 4)
    if decay is None:
        cp, c_blocks = None, None
    else:
        cp = prep(decay.astype(jnp.float32))
        c_blocks = cp.reshape(B, H, nb, BLOCK).transpose(2, 0, 1, 3)
    k_pos = jnp.arange(lp)
    k_valid = k_pos >= PAD

    def attend(args):
        i, qb, cb = args
        s = jnp.einsum('bhqd,bhkd->bhqk', qb, kp)
        if cp is not None:
            s = s + cb[..., :, None] - cp[:, :, None, :]
        q_pos = i * BLOCK + jnp.arange(BLOCK)
        mask = (k_pos[None, :] <= q_pos[:, None]) & k_valid[None, :]
        s = jnp.where(mask[None, None], s, NEG)
        p = jax.nn.softmax(s, axis=-1)
        return jnp.einsum('bhqk,bhkd->bhqd', p.astype(vp.dtype), vp)

    out = lax.map(attend, (jnp.arange(nb), q_blocks, c_blocks))
    out = out.transpose(1, 0, 3, 2, 4).reshape(B, lp, H * dv)
    return out[:, PAD:]


def mixer_block(u, cos, sin, w_in, b_forget, q_a_norm, w_q_up, kv_a_norm, w_kv_up,
                w_mla_out, w_fox_out, w_out):
    B, L, _ = u.shape
    proj = u @ w_in
    offsets = [int(o) for o in np.cumsum(IN_SPLITS)[:-1]]
    c_q, c_kv, k_r, fq, fk, fv, f_logit, g_a, g_b = jnp.split(proj, offsets, axis=-1)

    q = (rms_norm(c_q, q_a_norm) @ w_q_up).reshape(B, L, MLA_HEADS, MLA_NOPE + MLA_ROPE)
    q_mla = jnp.concatenate([q[..., :MLA_NOPE], apply_rope(q[..., MLA_NOPE:], cos, sin)], axis=-1)
    kv = (rms_norm(c_kv, kv_a_norm) @ w_kv_up).reshape(B, L, MLA_HEADS, MLA_NOPE + MLA_V)
    k_rope = jnp.broadcast_to(apply_rope(k_r[:, :, None, :], cos, sin), (B, L, MLA_HEADS, MLA_ROPE))
    k_mla = jnp.concatenate([kv[..., :MLA_NOPE], k_rope], axis=-1)
    y_a = blocked_causal_attention(q_mla, k_mla, kv[..., MLA_NOPE:], None,
                                   (MLA_NOPE + MLA_ROPE) ** -0.5)

    log_f = jax.nn.log_sigmoid(f_logit.astype(jnp.float32) + b_forget.astype(jnp.float32))
    decay = jnp.cumsum(log_f, axis=1)
    y_b = blocked_causal_attention(fq.reshape(B, L, FOX_HEADS, FOX_HD),
                                   fk.reshape(B, L, FOX_HEADS, FOX_HD),
                                   fv.reshape(B, L, FOX_HEADS, FOX_HD), decay, FOX_HD ** -0.5)

    merged = jax.nn.sigmoid(g_a) * (y_a @ w_mla_out) + jax.nn.sigmoid(g_b) * (y_b @ w_fox_out)
    return merged @ w_out


def hier_moe(u, w_group_router, b_group_router, w_expert_router, b_expert_router,
             w_gate, w_up, w_down):
    B, L, D = u.shape
    t = u.reshape(B * L, D)
    g_prob = jax.nn.softmax((t @ w_group_router).astype(jnp.float32) + b_group_router.astype(jnp.float32), axis=-1)
    g_w, g_idx = lax.top_k(g_prob, 1)
    e_logits = ((t @ w_expert_router).astype(jnp.float32) + b_expert_router.astype(jnp.float32)).reshape(-1, N_GROUPS, EXPERTS_PER_GROUP)
    e_logits = jnp.take_along_axis(e_logits, g_idx[:, :, None], axis=1)[:, 0]
    e_w, e_idx = lax.top_k(jax.nn.softmax(e_logits, axis=-1), TOP_K)
    e_w = e_w / jnp.sum(e_w, axis=-1, keepdims=True)
    within = jnp.einsum('tk,tke->te', e_w, jax.nn.one_hot(e_idx, EXPERTS_PER_GROUP, dtype=jnp.float32))
    combine = ((g_w * jax.nn.one_hot(g_idx[:, 0], N_GROUPS, dtype=jnp.float32))[:, :, None]
               * within[:, None, :]).astype(u.dtype)
    y = jnp.zeros_like(t)
    for g in range(N_GROUPS):
        a = jnp.einsum('td,edf->tef', t, w_gate[g])
        b = jnp.einsum('td,edf->tef', t, w_up[g])
        hmid = jax.nn.silu(a) * b * combine[:, g, :, None]
        y = y + jnp.einsum('tef,efd->td', hmid, w_down[g])
    return y.reshape(B, L, D)


def setup_inputs(seed: int = 0) -> dict:
    key = jax.random.key(seed)
    ks = jax.random.split(key, 21)
    f32 = jnp.float32
    nrm = lambda k, shape, scale: jax.random.normal(k, shape, f32) * scale
    G, E, F, D = N_GROUPS, EXPERTS_PER_GROUP, D_EXPERT, D_MODEL
    return {
        "x": nrm(ks[0], (BATCH, SEQ, D), 1.0),
        "meta": nrm(ks[1], (N_META, D), 1.0),
        "attn_norm": 1.0 + nrm(ks[2], (DEPTH, D), 0.02),
        "w_in": nrm(ks[3], (DEPTH, D, D_IN), D ** -0.5),
        "b_forget": 2.0 + nrm(ks[4], (DEPTH, FOX_HEADS), 0.1),
        "q_a_norm": 1.0 + nrm(ks[5], (DEPTH, Q_LORA), 0.02),
        "w_q_up": nrm(ks[6], (DEPTH, Q_LORA, MLA_HEADS * (MLA_NOPE + MLA_ROPE)), Q_LORA ** -0.5),
        "kv_a_norm": 1.0 + nrm(ks[7], (DEPTH, KV_LORA), 0.02),
        "w_kv_up": nrm(ks[8], (DEPTH, KV_LORA, MLA_HEADS * (MLA_NOPE + MLA_V)), KV_LORA ** -0.5),
        "w_mla_out": nrm(ks[9], (DEPTH, MLA_HEADS * MLA_V, D), (MLA_HEADS * MLA_V) ** -0.5),
        "w_fox_out": nrm(ks[10], (DEPTH, FOX_HEADS * FOX_HD, D), (FOX_HEADS * FOX_HD) ** -0.5),
        "w_out": nrm(ks[11], (DEPTH, D, D), D ** -0.5),
        "ffn_norm": 1.0 + nrm(ks[12], (DEPTH, D), 0.02),
        "w_group_router": nrm(ks[13], (DEPTH, D, G), D ** -0.5),
        "b_group_router": nrm(ks[14], (DEPTH, G), 0.01),
        "w_expert_router": nrm(ks[15], (DEPTH, D, G * E), D ** -0.5),
        "b_expert_router": nrm(ks[16], (DEPTH, G * E), 0.01),
        "w_gate": nrm(ks[17], (DEPTH, G, E, D, F), D ** -0.5),
        "w_up": nrm(ks[18], (DEPTH, G, E, D, F), D ** -0.5),
        "w_down": nrm(ks[19], (DEPTH, G, E, F, D), F ** -0.5),
        "final_norm": 1.0 + nrm(ks[20], (D,), 0.02),
    }


def reference(x, meta, attn_norm, w_in, b_forget, q_a_norm, w_q_up, kv_a_norm, w_kv_up,
              w_mla_out, w_fox_out, w_out, ffn_norm, w_group_router, b_group_router,
              w_expert_router, b_expert_router, w_gate, w_up, w_down, final_norm):
    B = x.shape[0]
    h = jnp.concatenate([jnp.broadcast_to(meta[None].astype(x.dtype), (B, N_META, D_MODEL)), x], axis=1)
    cos, sin = rope_tables(h.shape[1])
    for l in range(DEPTH):
        h = h + mixer_block(rms_norm(h, attn_norm[l]), cos, sin, w_in[l], b_forget[l],
                            q_a_norm[l], w_q_up[l], kv_a_norm[l], w_kv_up[l],
                            w_mla_out[l], w_fox_out[l], w_out[l])
        h = h + hier_moe(rms_norm(h, ffn_norm[l]), w_group_router[l], b_group_router[l],
                         w_expert_router[l], b_expert_router[l], w_gate[l], w_up[l], w_down[l])
    h = rms_norm(h, final_norm)
    return h[:, N_META:]
```

```python
import functools

import numpy as np
import jax
import jax.numpy as jnp
from jax import lax
from jax.experimental import pallas as pl
from jax.experimental.pallas import tpu as pltpu

F32 = jnp.float32
BF16 = jnp.bfloat16

D_MODEL = 1024
N_META = 16
HEADS = 8
MLA_NOPE, MLA_ROPE, MLA_V = 64, 32, 64
Q_LORA, KV_LORA = 512, 256
ROPE_THETA = 10000.0
FOX_HD = 64
N_GROUPS, EXPERTS_PER_GROUP, D_EXPERT = 4, 8, 256
N_EXPERTS = N_GROUPS * EXPERTS_PER_GROUP
EPS = 1e-6
NEG = -1e30
LOG2E = 1.4426950408889634

LANES = 128
HEAD_PAD = LANES
META_PAD = LANES
ROPE_LO = MLA_NOPE
ROPE_HALF = MLA_ROPE // 2
FOX_AUG = FOX_HD
N_SPLIT = 3

VMEM_LIMIT = 56 * 1024 * 1024

_NT = (((1,), (1,)), ((), ()))
_TN = (((0,), (0,)), ((), ()))


def _rms(x, g):
    return x * lax.rsqrt(jnp.mean(x * x, axis=-1, keepdims=True) + EPS) * g


def _split3(v):
    hi = v.astype(BF16)
    r = v - hi.astype(F32)
    mid = r.astype(BF16)
    lo = (r - mid.astype(F32)).astype(BF16)
    return hi, mid, lo


def _rope_tile(v, c, s):
    lane = lax.broadcasted_iota(jnp.int32, v.shape, 1)
    partner = jnp.where(lane < ROPE_LO + ROPE_HALF,
                        pltpu.roll(v, LANES - ROPE_HALF, 1), pltpu.roll(v, ROPE_HALF, 1))
    return v * c + partner * s


def _proj_kernel(x_ref, cos_ref, sin_ref, c0_ref, an_ref, wa_ref, qn_ref, wq_ref, kvn_ref, wk_ref,
                 wvt_ref, wfvt_ref, bf_ref, eq_ref, ek_ref, oq_ref, ok_ref,
                 qm_ref, km_ref, vm_ref, qf_ref, kf_ref, vf_ref, cl_ref, carry_ref, *, tm):
    i = pl.program_id(1)

    @pl.when(i == 0)
    def _():
        carry_ref[...] = jnp.broadcast_to(c0_ref[...], carry_ref.shape)

    ub = _rms(x_ref[0], an_ref[...]).astype(BF16)
    pa = jnp.dot(ub, wa_ref[...], preferred_element_type=F32)
    c_q = pa[:, 0:Q_LORA]
    c_kv = pa[:, Q_LORA:Q_LORA + KV_LORA]
    o = Q_LORA + KV_LORA
    kr = pa[:, o:o + LANES]
    fl = pa[:, o + LANES:o + 2 * LANES]
    o += 2 * LANES
    fq = pa[:, o:o + HEADS * HEAD_PAD]
    fk = pa[:, o + HEADS * HEAD_PAD:o + 2 * HEADS * HEAD_PAD]

    cos = cos_ref[...]
    sin = sin_ref[...]

    cqn = _rms(c_q, qn_ref[...]).astype(BF16)
    q = jnp.dot(cqn, wq_ref[...], preferred_element_type=F32)
    ckvn = _rms(c_kv, kvn_ref[...]).astype(BF16)
    kb = jnp.dot(ckvn, wk_ref[...], preferred_element_type=F32)
    krr = _rope_tile(kr, cos, sin)
    q_scale = (MLA_NOPE + MLA_ROPE) ** -0.5 * LOG2E
    for h in range(HEADS):
        sl = slice(h * HEAD_PAD, (h + 1) * HEAD_PAD)
        qm_ref[0, h] = (_rope_tile(q[:, sl], cos, sin) * q_scale).astype(BF16)
        km_ref[0, h] = (kb[:, sl] + krr).astype(BF16)
    vt = lax.dot_general(wvt_ref[...], ckvn, _NT, preferred_element_type=F32)
    for h in range(HEADS):
        vm_ref[0, h, 0] = vt[h * MLA_V:(h + 1) * MLA_V, :].astype(BF16)

    z = fl + bf_ref[...]
    logf = jnp.minimum(z, 0.0) - jnp.log1p(jnp.exp(-jnp.abs(z)))
    r_i = lax.broadcasted_iota(jnp.int32, (tm, tm), 0)
    c_i = lax.broadcasted_iota(jnp.int32, (tm, tm), 1)
    tri = jnp.where(c_i <= r_i, 1.0, 0.0).astype(BF16)
    cs = sum(jnp.dot(tri, piece, preferred_element_type=F32) for piece in _split3(logf))
    c = carry_ref[0:1, :] + cs
    carry_ref[...] = jnp.broadcast_to(c[tm - 1:tm, :], carry_ref.shape)
    cl_ref[0] = c
    cat = jnp.concatenate(_split3(c * LOG2E), axis=1)
    augq = jnp.dot(cat, eq_ref[...], preferred_element_type=F32)
    augk = jnp.dot(cat, ek_ref[...], preferred_element_type=F32)
    qfull = fq * (FOX_HD ** -0.5 * LOG2E) + augq + oq_ref[...]
    kfull = fk + augk + ok_ref[...]
    for h in range(HEADS):
        sl = slice(h * HEAD_PAD, (h + 1) * HEAD_PAD)
        qf_ref[0, h] = qfull[:, sl].astype(BF16)
        kf_ref[0, h] = kfull[:, sl].astype(BF16)
    vft = lax.dot_general(wfvt_ref[...], ub, _NT, preferred_element_type=F32)
    for h in range(HEADS):
        vf_ref[0, h, 0] = vft[h * FOX_HD:(h + 1) * FOX_HD, :].astype(BF16)


def _projections(x3, cos, sin, c0, consts, *, tm):
    b, l, _ = x3.shape
    nt = l // tm
    full = lambda a: pl.BlockSpec(a.shape, lambda bi, i: (0,) * a.ndim)
    qk_shape = jax.ShapeDtypeStruct((b, HEADS, l, HEAD_PAD), BF16)
    vt_shape = jax.ShapeDtypeStruct((b, HEADS, nt, MLA_V, tm), BF16)
    qk_spec = pl.BlockSpec((1, HEADS, tm, HEAD_PAD), lambda bi, i: (bi, 0, i, 0))
    vt_spec = pl.BlockSpec((1, HEADS, 1, MLA_V, tm), lambda bi, i: (bi, 0, i, 0, 0))
    return pl.pallas_call(
        functools.partial(_proj_kernel, tm=tm),
        grid=(b, nt),
        in_specs=[pl.BlockSpec((1, tm, D_MODEL), lambda bi, i: (bi, i, 0)),
                  pl.BlockSpec((tm, LANES), lambda bi, i: (i, 0)),
                  pl.BlockSpec((tm, LANES), lambda bi, i: (i, 0)),
                  full(c0)] + [full(a) for a in consts],
        out_specs=[qk_spec, qk_spec, vt_spec, qk_spec, qk_spec, vt_spec,
                   pl.BlockSpec((1, tm, LANES), lambda bi, i: (bi, i, 0))],
        out_shape=[qk_shape, qk_shape, vt_shape, qk_shape, qk_shape, vt_shape,
                   jax.ShapeDtypeStruct((b, l, LANES), F32)],
        scratch_shapes=[pltpu.VMEM((8, LANES), F32)],
        compiler_params=pltpu.CompilerParams(
            dimension_semantics=("arbitrary", "arbitrary"), vmem_limit_bytes=VMEM_LIMIT),
        name="proj",
    )(x3, cos, sin, c0, *consts)


def _attn_kernel(q_ref, k_ref, vt_ref, km_ref, vtm_ref, o_ref, m_ref, l_ref, acc_ref, *, t):
    i = pl.program_id(2)
    q = q_ref[0, 0]

    def update(s, vt_blk, first=False):
        m_cur = jnp.max(s, axis=0, keepdims=True)
        if first:
            m_new = m_cur
        else:
            m_prev = m_ref[...]
            m_new = jnp.maximum(m_prev, m_cur)
            alpha = jnp.exp2(m_prev - m_new)
        p = jnp.exp2(s - m_new)
        psum = jnp.sum(p, axis=0, keepdims=True)
        pv = jnp.dot(vt_blk, p.astype(BF16), preferred_element_type=F32)
        if first:
            l_ref[...] = psum
            acc_ref[...] = pv
        else:
            l_ref[...] = alpha * l_ref[...] + psum
            acc_ref[...] = alpha * acc_ref[...] + pv
        m_ref[...] = m_new

    s = lax.dot_general(km_ref[0, 0], q, _NT, preferred_element_type=F32)
    row = lax.broadcasted_iota(jnp.int32, s.shape, 0)
    update(jnp.where(row < N_META, s, NEG), vtm_ref[0, 0, 0], first=True)

    def body(j, carry):
        k = k_ref[0, 0, pl.ds(pl.multiple_of(j * t, t), t), :]
        update(lax.dot_general(k, q, _NT, preferred_element_type=F32), vt_ref[0, 0, j])
        return carry

    lax.fori_loop(0, i, body, 0)

    k = k_ref[0, 0, pl.ds(pl.multiple_of(i * t, t), t), :]
    s = lax.dot_general(k, q, _NT, preferred_element_type=F32)
    kr = lax.broadcasted_iota(jnp.int32, s.shape, 0)
    qc = lax.broadcasted_iota(jnp.int32, s.shape, 1)
    update(jnp.where(kr <= qc, s, NEG), vt_ref[0, 0, i])

    o_ref[0] = (acc_ref[...] / l_ref[...]).astype(o_ref.dtype)


def _attention(q, k, vt, k_meta, vt_meta, *, t):
    b, h, l, _ = q.shape
    nq = l // t
    dv = vt.shape[3]
    return pl.pallas_call(
        functools.partial(_attn_kernel, t=t),
        grid=(b, h, nq),
        in_specs=[pl.BlockSpec((1, 1, t, HEAD_PAD), lambda bi, hi, i: (bi, hi, i, 0)),
                  pl.BlockSpec((1, 1, l, HEAD_PAD), lambda bi, hi, i: (bi, hi, 0, 0)),
                  pl.BlockSpec((1, 1, nq, dv, t), lambda bi, hi, i: (bi, hi, 0, 0, 0)),
                  pl.BlockSpec((1, 1, META_PAD, HEAD_PAD), lambda bi, hi, i: (0, hi, 0, 0)),
                  pl.BlockSpec((1, 1, 1, dv, META_PAD), lambda bi, hi, i: (0, hi, 0, 0, 0))],
        out_specs=pl.BlockSpec((1, dv, t), lambda bi, hi, i: (bi, hi, i)),
        out_shape=jax.ShapeDtypeStruct((b, h * dv, l), BF16),
        scratch_shapes=[pltpu.VMEM((1, t), F32), pltpu.VMEM((1, t), F32), pltpu.VMEM((dv, t), F32)],
        compiler_params=pltpu.CompilerParams(
            dimension_semantics=("arbitrary", "arbitrary", "arbitrary"), vmem_limit_bytes=VMEM_LIMIT),
        name="attn",
    )(q, k, vt, k_meta, vt_meta)


def _merge_kernel(x_ref, ya_ref, yb_ref, an_ref, wg_ref, wmo_ref, wfo_ref, wo_ref, fn_ref,
                  wrh_ref, wrl_ref, br_ref, h1_ref, t_ref, comb_ref):
    x = x_ref[0]
    ub = _rms(x, an_ref[...]).astype(BF16)
    g = jnp.dot(ub, wg_ref[...], preferred_element_type=F32)
    ya = lax.dot_general(ya_ref[0], wmo_ref[...], _TN, preferred_element_type=F32)
    yb = lax.dot_general(yb_ref[0], wfo_ref[...], _TN, preferred_element_type=F32)
    merged = jax.nn.sigmoid(g[:, :D_MODEL]) * ya + jax.nn.sigmoid(g[:, D_MODEL:]) * yb
    h1 = x + jnp.dot(merged.astype(BF16), wo_ref[...], preferred_element_type=F32)
    h1_ref[0] = h1
    t = _rms(h1, fn_ref[...])
    t_ref[0] = t.astype(BF16)

    th = t.astype(BF16)
    tl = (t - th.astype(F32)).astype(BF16)
    logits = (jnp.dot(th, wrh_ref[...], preferred_element_type=F32)
              + jnp.dot(tl, wrh_ref[...], preferred_element_type=F32)
              + jnp.dot(th, wrl_ref[...], preferred_element_type=F32)) + br_ref[...]
    lane = lax.broadcasted_iota(jnp.int32, logits.shape, 1)
    big = jnp.int32(LANES)

    def first_argmax(v, vmax):
        return jnp.min(jnp.where(v == vmax, lane, big), axis=-1, keepdims=True)

    gmask = (lane >= N_EXPERTS) & (lane < N_EXPERTS + N_GROUPS)
    gl = jnp.where(gmask, logits, NEG)
    gmax = jnp.max(gl, axis=-1, keepdims=True)
    g_w = 1.0 / jnp.sum(jnp.where(gmask, jnp.exp(gl - gmax), 0.0), axis=-1, keepdims=True)
    g_idx = first_argmax(gl, gmax) - N_EXPERTS
    emask = (lane < N_EXPERTS) & (jnp.right_shift(lane, 3) == g_idx)
    el = jnp.where(emask, logits, NEG)
    m1 = jnp.max(el, axis=-1, keepdims=True)
    i1 = first_argmax(el, m1)
    el2 = jnp.where(lane == i1, NEG, el)
    m2 = jnp.max(el2, axis=-1, keepdims=True)
    i2 = first_argmax(el2, m2)
    zsum = jnp.sum(jnp.where(emask, jnp.exp(el - m1), 0.0), axis=-1, keepdims=True)
    p1 = 1.0 / zsum
    p2 = jnp.exp(m2 - m1) / zsum
    tot = p1 + p2
    within = jnp.where(lane == i1, p1 / tot, 0.0) + jnp.where(lane == i2, p2 / tot, 0.0)
    comb_ref[0] = g_w * within


def _merge(x3, ya_t, yb_t, consts, *, tm):
    b, l, _ = x3.shape
    full = lambda a: pl.BlockSpec(a.shape, lambda bi, i: (0,) * a.ndim)
    tok = lambda w: pl.BlockSpec((1, tm, w), lambda bi, i: (bi, i, 0))
    yspec = pl.BlockSpec((1, ya_t.shape[1], tm), lambda bi, i: (bi, 0, i))
    return pl.pallas_call(
        _merge_kernel,
        grid=(b, l // tm),
        in_specs=[tok(D_MODEL), yspec, yspec] + [full(a) for a in consts],
        out_specs=[tok(D_MODEL), tok(D_MODEL), tok(LANES)],
        out_shape=[jax.ShapeDtypeStruct((b, l, D_MODEL), F32),
                   jax.ShapeDtypeStruct((b, l, D_MODEL), BF16),
                   jax.ShapeDtypeStruct((b, l, LANES), F32)],
        compiler_params=pltpu.CompilerParams(
            dimension_semantics=("arbitrary", "arbitrary"), vmem_limit_bytes=VMEM_LIMIT),
        name="merge",
    )(x3, ya_t, yb_t, *consts)


def _moe_kernel(t_ref, comb_ref, h1_ref, wg_ref, wu_ref, wd_ref, fn_ref, o_ref, acc_ref):
    e = pl.program_id(1)

    @pl.when(e == 0)
    def _():
        acc_ref[...] = jnp.zeros_like(acc_ref)

    tb = t_ref[...]
    a = jnp.dot(tb, wg_ref[0].astype(BF16), preferred_element_type=F32)
    u = jnp.dot(tb, wu_ref[0].astype(BF16), preferred_element_type=F32)
    comb = comb_ref[...]
    lane = lax.broadcasted_iota(jnp.int32, comb.shape, 1)
    ce = jnp.sum(jnp.where(lane == e, comb, 0.0), axis=-1, keepdims=True)
    hm = (a * jax.nn.sigmoid(a)) * u * ce
    acc_ref[...] += jnp.dot(hm.astype(BF16), wd_ref[0].astype(BF16), preferred_element_type=F32)

    @pl.when(e == pl.num_programs(1) - 1)
    def _():
        o_ref[...] = _rms(h1_ref[...] + acc_ref[...], fn_ref[...])


def _moe(t2, comb2, h2, w_gate, w_up, w_down, final_norm, *, tm):
    n, _ = t2.shape
    tok = lambda w: pl.BlockSpec((tm, w), lambda i, e: (i, 0))
    return pl.pallas_call(
        _moe_kernel,
        grid=(n // tm, N_EXPERTS),
        in_specs=[tok(D_MODEL), tok(LANES), tok(D_MODEL),
                  pl.BlockSpec((1, D_MODEL, D_EXPERT), lambda i, e: (e, 0, 0)),
                  pl.BlockSpec((1, D_MODEL, D_EXPERT), lambda i, e: (e, 0, 0)),
                  pl.BlockSpec((1, D_EXPERT, D_MODEL), lambda i, e: (e, 0, 0)),
                  pl.BlockSpec((1, D_MODEL), lambda i, e: (0, 0))],
        out_specs=tok(D_MODEL),
        out_shape=jax.ShapeDtypeStruct((n, D_MODEL), F32),
        scratch_shapes=[pltpu.VMEM((tm, D_MODEL), F32)],
        compiler_params=pltpu.CompilerParams(
            dimension_semantics=("arbitrary", "arbitrary"), vmem_limit_bytes=VMEM_LIMIT),
        name="moe",
    )(t2, comb2, h2, w_gate, w_up, w_down, final_norm)


def _pad_heads(w, width):
    k = w.shape[0]
    return jnp.pad(w.reshape(k, HEADS, width), ((0, 0), (0, 0), (0, HEAD_PAD - width))).reshape(k, HEADS * HEAD_PAD)


def _rope_tables(length):
    pos = jnp.arange(length, dtype=F32)
    inv = ROPE_THETA ** (-jnp.arange(0, MLA_ROPE, 2, dtype=F32) / MLA_ROPE)
    ang = pos[:, None] * inv[None, :]
    cos, sin = jnp.cos(ang), jnp.sin(ang)
    one = jnp.ones((length, ROPE_LO), F32)
    zero_lo = jnp.zeros((length, ROPE_LO), F32)
    zero_hi = jnp.zeros((length, LANES - ROPE_LO - MLA_ROPE), F32)
    return (jnp.concatenate([one, cos, cos, zero_hi], axis=1),
            jnp.concatenate([zero_lo, -sin, sin, zero_hi], axis=1))


def _decay_columns():
    eq = np.zeros((N_SPLIT * LANES, HEADS * HEAD_PAD), np.float32)
    ek = np.zeros_like(eq)
    oq = np.zeros((1, HEADS * HEAD_PAD), np.float32)
    ok = np.zeros_like(oq)
    for h in range(HEADS):
        for p in range(N_SPLIT):
            eq[p * LANES + h, h * HEAD_PAD + FOX_AUG + p] = 1.0
            ek[p * LANES + h, h * HEAD_PAD + FOX_AUG + N_SPLIT + p] = -1.0
            oq[0, h * HEAD_PAD + FOX_AUG + N_SPLIT + p] = 1.0
            ok[0, h * HEAD_PAD + FOX_AUG + p] = 1.0
    return jnp.asarray(eq, BF16), jnp.asarray(ek, BF16), jnp.asarray(oq), jnp.asarray(ok)


def kernel(x, meta, attn_norm, w_in, b_forget, q_a_norm, w_q_up, kv_a_norm, w_kv_up, w_mla_out, w_fox_out, w_out, ffn_norm, w_group_router, b_group_router, w_expert_router, b_expert_router, w_gate, w_up, w_down, final_norm):
    b, seq, d = x.shape
    t_attn = 512
    row = lambda v: v.reshape(1, -1).astype(F32)

    w = w_in[0]
    offs = np.cumsum([0, Q_LORA, KV_LORA, MLA_ROPE, HEADS * FOX_HD, HEADS * FOX_HD, HEADS * FOX_HD,
                      HEADS, D_MODEL, D_MODEL])
    seg = lambda j: w[:, offs[j]:offs[j + 1]]
    zcol = lambda n: jnp.zeros((d, n), F32)
    w_a = jnp.concatenate(
        [seg(0), seg(1),
         zcol(ROPE_LO), seg(2), zcol(LANES - ROPE_LO - MLA_ROPE),
         seg(6), zcol(LANES - HEADS),
         _pad_heads(seg(3), FOX_HD), _pad_heads(seg(4), FOX_HD)], axis=1).astype(BF16)
    w_fvt = seg(5).T.astype(BF16)
    w_gates = jnp.concatenate([seg(7), seg(8)], axis=1).astype(BF16)
    w_q = _pad_heads(w_q_up[0], MLA_NOPE + MLA_ROPE).astype(BF16)
    kv3 = w_kv_up[0].reshape(KV_LORA, HEADS, MLA_NOPE + MLA_V)
    w_k = _pad_heads(kv3[:, :, :MLA_NOPE].reshape(KV_LORA, HEADS * MLA_NOPE), MLA_NOPE).astype(BF16)
    w_vt = kv3[:, :, MLA_NOPE:].reshape(KV_LORA, HEADS * MLA_V).T.astype(BF16)
    bf128 = jnp.pad(row(b_forget[0]), ((0, 0), (0, LANES - HEADS)))
    eq, ek, oq, ok = _decay_columns()
    cos, sin = _rope_tables(N_META + seq)
    consts = [row(attn_norm[0]), w_a, row(q_a_norm[0]), w_q, row(kv_a_norm[0]), w_k, w_vt, w_fvt, bf128,
              eq, ek, oq, ok]

    meta_pad = jnp.pad(meta.astype(F32), ((0, META_PAD - N_META), (0, 0)))[None]
    pad_tab = lambda tab: jnp.pad(tab[:N_META], ((0, META_PAD - N_META), (0, 0)))
    _, km_meta, vm_meta, _, kf_meta, vf_meta, c_meta = _projections(
        meta_pad, pad_tab(cos), pad_tab(sin), jnp.zeros((1, LANES), F32), consts, tm=META_PAD)
    c0 = c_meta[0, N_META - 1:N_META, :]

    qm, km, vm, qf, kf, vf, _ = _projections(x, cos[N_META:], sin[N_META:], c0, consts, tm=t_attn)
    ya_t = _attention(qm, km, vm, km_meta, vm_meta, t=t_attn)
    yb_t = _attention(qf, kf, vf, kf_meta, vf_meta, t=t_attn)

    w_r = jnp.concatenate([w_expert_router[0], w_group_router[0],
                           jnp.zeros((d, LANES - N_EXPERTS - N_GROUPS), F32)], axis=1)
    w_rh = w_r.astype(BF16)
    w_rl = (w_r - w_rh.astype(F32)).astype(BF16)
    b_r = jnp.pad(jnp.concatenate([row(b_expert_router[0]), row(b_group_router[0])], axis=1),
                  ((0, 0), (0, LANES - N_EXPERTS - N_GROUPS)))
    h1, t, comb = _merge(x, ya_t, yb_t,
                         [row(attn_norm[0]), w_gates, w_mla_out[0].astype(BF16), w_fox_out[0].astype(BF16),
                          w_out[0].astype(BF16), row(ffn_norm[0]), w_rh, w_rl, b_r], tm=512)

    n = b * seq
    out = _moe(t.reshape(n, d), comb.reshape(n, LANES), h1.reshape(n, d),
               w_gate[0].reshape(N_EXPERTS, d, D_EXPERT), w_up[0].reshape(N_EXPERTS, d, D_EXPERT),
               w_down[0].reshape(N_EXPERTS, D_EXPERT, d), row(final_norm), tm=1024)
    return out.reshape(b, seq, d)
```

```python
import functools

import numpy as np
import jax
import jax.numpy as jnp
from jax import lax
from jax.experimental import pallas as pl
from jax.experimental.pallas import tpu as pltpu

F32 = jnp.float32
BF16 = jnp.bfloat16

D_MODEL = 1024
N_META = 16
HEADS = 8
MLA_NOPE, MLA_ROPE, MLA_V = 64, 32, 64
Q_LORA, KV_LORA = 512, 256
ROPE_THETA = 10000.0
FOX_HD = 64
N_GROUPS, EXPERTS_PER_GROUP, D_EXPERT = 4, 8, 256
N_EXPERTS = N_GROUPS * EXPERTS_PER_GROUP
EPS = 1e-6
NEG = -1e30
LOG2E = 1.4426950408889634

LANES = 128
HEAD_PAD = LANES
META_PAD = LANES
ROPE_LO = MLA_NOPE
ROPE_HALF = MLA_ROPE // 2
FOX_AUG = FOX_HD
N_SPLIT = 3

VMEM_LIMIT = 56 * 1024 * 1024
ATTN_HEADS_PER_STEP = 8
SKEW = 2
ACC_PAD = 16
ROW_TILE = D_MODEL // LANES
ROW_PITCH = ROW_TILE + 1
EXPERT_ROWS = 256
COMBINE_TOKENS = 256

_NT = (((1,), (1,)), ((), ()))
_TN = (((0,), (0,)), ((), ()))


def _rms(x, g):
    return x * lax.rsqrt(jnp.mean(x * x, axis=-1, keepdims=True) + EPS) * g


def _rows_to_tiles(ref2d, v):
    for c in range(ROW_TILE):
        ref2d[pl.ds(c, v.shape[0], stride=ROW_TILE), :] = v[:, c * LANES:(c + 1) * LANES]


def _rows_from_tiles(ref2d, first, n, pitch):
    return jnp.concatenate([ref2d[pl.ds(first + c, n, stride=pitch), :] for c in range(ROW_TILE)], axis=1)


def _split3(v):
    hi = v.astype(BF16)
    r = v - hi.astype(F32)
    mid = r.astype(BF16)
    lo = (r - mid.astype(F32)).astype(BF16)
    return hi, mid, lo


def _rope_tile(v, c, s):
    lane = lax.broadcasted_iota(jnp.int32, v.shape, 1)
    partner = jnp.where(lane < ROPE_LO + ROPE_HALF,
                        pltpu.roll(v, LANES - ROPE_HALF, 1), pltpu.roll(v, ROPE_HALF, 1))
    return v * c + partner * s


def _proj_kernel(x_ref, cos_ref, sin_ref, c0_ref, an_ref, wa_ref, qn_ref, wq_ref, kvn_ref, wk_ref,
                 wvt_ref, wfvt_ref, bf_ref, eq_ref, ek_ref, oq_ref, ok_ref,
                 qm_ref, km_ref, vm_ref, qf_ref, kf_ref, vf_ref, cl_ref, carry_ref, *, tm):
    i = pl.program_id(1)

    @pl.when(i == 0)
    def _():
        carry_ref[...] = jnp.broadcast_to(c0_ref[...], carry_ref.shape)

    ub = _rms(x_ref[0], an_ref[...]).astype(BF16)
    pa = jnp.dot(ub, wa_ref[...], preferred_element_type=F32)
    c_q = pa[:, 0:Q_LORA]
    c_kv = pa[:, Q_LORA:Q_LORA + KV_LORA]
    o = Q_LORA + KV_LORA
    kr = pa[:, o:o + LANES]
    fl = pa[:, o + LANES:o + 2 * LANES]
    o += 2 * LANES
    fq = pa[:, o:o + HEADS * HEAD_PAD]
    fk = pa[:, o + HEADS * HEAD_PAD:o + 2 * HEADS * HEAD_PAD]

    cos = cos_ref[...]
    sin = sin_ref[...]

    cqn = _rms(c_q, qn_ref[...]).astype(BF16)
    q = jnp.dot(cqn, wq_ref[...], preferred_element_type=F32)
    ckvn = _rms(c_kv, kvn_ref[...]).astype(BF16)
    kb = jnp.dot(ckvn, wk_ref[...], preferred_element_type=F32)
    krr = _rope_tile(kr, cos, sin)
    q_scale = (MLA_NOPE + MLA_ROPE) ** -0.5 * LOG2E
    for h in range(HEADS):
        sl = slice(h * HEAD_PAD, (h + 1) * HEAD_PAD)
        qm_ref[0, h] = (_rope_tile(q[:, sl], cos, sin) * q_scale).astype(BF16)
        km_ref[0, h] = (kb[:, sl] + krr).astype(BF16)
    vt = lax.dot_general(wvt_ref[...], ckvn, _NT, preferred_element_type=F32)
    for h in range(HEADS):
        vm_ref[0, h, 0] = vt[h * MLA_V:(h + 1) * MLA_V, :].astype(BF16)

    z = fl + bf_ref[...]
    logf = jnp.minimum(z, 0.0) - jnp.log1p(jnp.exp(-jnp.abs(z)))
    r_i = lax.broadcasted_iota(jnp.int32, (tm, tm), 0)
    c_i = lax.broadcasted_iota(jnp.int32, (tm, tm), 1)
    tri = jnp.where(c_i <= r_i, 1.0, 0.0).astype(BF16)
    cs = sum(jnp.dot(tri, piece, preferred_element_type=F32) for piece in _split3(logf))
    c = carry_ref[0:1, :] + cs
    carry_ref[...] = jnp.broadcast_to(c[tm - 1:tm, :], carry_ref.shape)
    cl_ref[0] = c
    cat = jnp.concatenate(_split3(c * LOG2E), axis=1)
    augq = jnp.dot(cat, eq_ref[...], preferred_element_type=F32)
    augk = jnp.dot(cat, ek_ref[...], preferred_element_type=F32)
    qfull = fq * (FOX_HD ** -0.5 * LOG2E) + augq + oq_ref[...]
    kfull = fk + augk + ok_ref[...]
    for h in range(HEADS):
        sl = slice(h * HEAD_PAD, (h + 1) * HEAD_PAD)
        qf_ref[0, h] = qfull[:, sl].astype(BF16)
        kf_ref[0, h] = kfull[:, sl].astype(BF16)
    vft = lax.dot_general(wfvt_ref[...], ub, _NT, preferred_element_type=F32)
    for h in range(HEADS):
        vf_ref[0, h, 0] = vft[h * FOX_HD:(h + 1) * FOX_HD, :].astype(BF16)


def _projections(x3, cos, sin, c0, consts, *, tm):
    b, l, _ = x3.shape
    nt = l // tm
    full = lambda a: pl.BlockSpec(a.shape, lambda bi, i: (0,) * a.ndim)
    qk_shape = jax.ShapeDtypeStruct((b, HEADS, l, HEAD_PAD), BF16)
    vt_shape = jax.ShapeDtypeStruct((b, HEADS, nt, MLA_V, tm), BF16)
    qk_spec = pl.BlockSpec((1, HEADS, tm, HEAD_PAD), lambda bi, i: (bi, 0, i, 0))
    vt_spec = pl.BlockSpec((1, HEADS, 1, MLA_V, tm), lambda bi, i: (bi, 0, i, 0, 0))
    return pl.pallas_call(
        functools.partial(_proj_kernel, tm=tm),
        grid=(b, nt),
        in_specs=[pl.BlockSpec((1, tm, D_MODEL), lambda bi, i: (bi, i, 0)),
                  pl.BlockSpec((tm, LANES), lambda bi, i: (i, 0)),
                  pl.BlockSpec((tm, LANES), lambda bi, i: (i, 0)),
                  full(c0)] + [full(a) for a in consts],
        out_specs=[qk_spec, qk_spec, vt_spec, qk_spec, qk_spec, vt_spec,
                   pl.BlockSpec((1, tm, LANES), lambda bi, i: (bi, i, 0))],
        out_shape=[qk_shape, qk_shape, vt_shape, qk_shape, qk_shape, vt_shape,
                   jax.ShapeDtypeStruct((b, l, LANES), F32)],
        scratch_shapes=[pltpu.VMEM((8, LANES), F32)],
        compiler_params=pltpu.CompilerParams(
            dimension_semantics=("arbitrary", "arbitrary"), vmem_limit_bytes=VMEM_LIMIT),
        name="proj",
    )(x3, cos, sin, c0, *consts)


def _attn_kernel(q_ref, k_ref, vt_ref, km_ref, vtm_ref, o_ref, m_ref, acc_ref, *, t, g):
    i = pl.program_id(2)
    dv = vt_ref.shape[3]

    ones_rows = jnp.where(lax.broadcasted_iota(jnp.int32, (ACC_PAD, t), 0) == 0, 1.0, 0.0).astype(BF16)

    def update(hd, s, vt_blk, first=False):
        m_cur = jnp.max(s, axis=0, keepdims=True)
        if first:
            m_new = m_cur
        else:
            m_prev = m_ref[hd]
            m_new = jnp.maximum(m_prev, m_cur)
            alpha = jnp.exp2(m_prev - m_new)
        p = jnp.exp2(s - m_new).astype(BF16)
        vt_aug = jnp.concatenate([vt_blk, ones_rows[:, :p.shape[0]]], axis=0)
        pv = jnp.dot(vt_aug, p, preferred_element_type=F32)
        acc_ref[hd] = pv if first else alpha * acc_ref[hd] + pv
        m_ref[hd] = m_new

    def scores(hd, k):
        return lax.dot_general(k, q_ref[0, hd], _NT, preferred_element_type=F32)

    def keys(j):
        off = pl.multiple_of(j * t, t)
        return lambda hd: k_ref[0, hd, pl.ds(off, t), :]

    def sweep(keys_of, vt_of, mask, first=False):
        pending = [scores(hd, keys_of(hd)) for hd in range(SKEW)]
        for hd in range(g):
            if hd + SKEW < g:
                pending.append(scores(hd + SKEW, keys_of(hd + SKEW)))
            s = pending.pop(0)
            if mask is not None:
                s = jnp.where(mask, s, NEG)
            update(hd, s, vt_of(hd), first=first)

    row = lax.broadcasted_iota(jnp.int32, (META_PAD, t), 0)
    sweep(lambda hd: km_ref[0, hd], lambda hd: vtm_ref[0, hd, 0], row < N_META, first=True)

    def body(j, carry):
        sweep(keys(j), lambda hd: vt_ref[0, hd, j], None)
        return carry

    lax.fori_loop(0, i, body, 0)

    causal = (lax.broadcasted_iota(jnp.int32, (t, t), 0) <= lax.broadcasted_iota(jnp.int32, (t, t), 1))
    sweep(keys(i), lambda hd: vt_ref[0, hd, i], causal)

    for hd in range(g):
        a = acc_ref[hd]
        o_ref[0, hd * dv:(hd + 1) * dv, :] = (a[:dv] / a[dv:dv + 1]).astype(o_ref.dtype)


def _attention(q, k, vt, k_meta, vt_meta, *, t, g):
    b, h, l, _ = q.shape
    nq = l // t
    dv = vt.shape[3]
    return pl.pallas_call(
        functools.partial(_attn_kernel, t=t, g=g),
        grid=(b, h // g, nq),
        in_specs=[pl.BlockSpec((1, g, t, HEAD_PAD), lambda bi, hi, i: (bi, hi, i, 0)),
                  pl.BlockSpec((1, g, l, HEAD_PAD), lambda bi, hi, i: (bi, hi, 0, 0),
                               pipeline_mode=pl.Buffered(1)),
                  pl.BlockSpec((1, g, nq, dv, t), lambda bi, hi, i: (bi, hi, 0, 0, 0),
                               pipeline_mode=pl.Buffered(1)),
                  pl.BlockSpec((1, g, META_PAD, HEAD_PAD), lambda bi, hi, i: (0, hi, 0, 0)),
                  pl.BlockSpec((1, g, 1, dv, META_PAD), lambda bi, hi, i: (0, hi, 0, 0, 0))],
        out_specs=pl.BlockSpec((1, g * dv, t), lambda bi, hi, i: (bi, hi, i)),
        out_shape=jax.ShapeDtypeStruct((b, h * dv, l), BF16),
        scratch_shapes=[pltpu.VMEM((g, 1, t), F32), pltpu.VMEM((g, dv + ACC_PAD, t), F32)],
        compiler_params=pltpu.CompilerParams(
            dimension_semantics=("arbitrary", "arbitrary", "arbitrary"), vmem_limit_bytes=VMEM_LIMIT),
        name="attn",
    )(q, k, vt, k_meta, vt_meta)


def _merge_kernel(x_ref, ya_ref, yb_ref, an_ref, wg_ref, wmo_ref, wfo_ref, wo_ref, fn_ref,
                  wrh_ref, wrl_ref, br_ref, h1_ref, t_ref, ri_ref, rw_ref, cnt_ref):
    @pl.when((pl.program_id(0) == 0) & (pl.program_id(1) == 0))
    def _():
        cnt_ref[...] = jnp.zeros_like(cnt_ref)

    x = x_ref[0]
    ub = _rms(x, an_ref[...]).astype(BF16)
    g = jnp.dot(ub, wg_ref[...], preferred_element_type=F32)
    ya = lax.dot_general(ya_ref[0], wmo_ref[...], _TN, preferred_element_type=F32)
    yb = lax.dot_general(yb_ref[0], wfo_ref[...], _TN, preferred_element_type=F32)
    merged = jax.nn.sigmoid(g[:, :D_MODEL]) * ya + jax.nn.sigmoid(g[:, D_MODEL:]) * yb
    h1 = x + jnp.dot(merged.astype(BF16), wo_ref[...], preferred_element_type=F32)
    h1_ref[0] = h1
    t = _rms(h1, fn_ref[...])
    _rows_to_tiles(t_ref, t)

    th = t.astype(BF16)
    tl = (t - th.astype(F32)).astype(BF16)
    logits = (jnp.dot(th, wrh_ref[...], preferred_element_type=F32)
              + jnp.dot(tl, wrh_ref[...], preferred_element_type=F32)
              + jnp.dot(th, wrl_ref[...], preferred_element_type=F32)) + br_ref[...]
    lane = lax.broadcasted_iota(jnp.int32, logits.shape, 1)
    big = jnp.int32(LANES)

    def first_argmax(v, vmax):
        return jnp.min(jnp.where(v == vmax, lane, big), axis=-1, keepdims=True)

    gmask = (lane >= N_EXPERTS) & (lane < N_EXPERTS + N_GROUPS)
    gl = jnp.where(gmask, logits, NEG)
    gmax = jnp.max(gl, axis=-1, keepdims=True)
    g_w = 1.0 / jnp.sum(jnp.where(gmask, jnp.exp(gl - gmax), 0.0), axis=-1, keepdims=True)
    g_idx = first_argmax(gl, gmax) - N_EXPERTS
    emask = (lane < N_EXPERTS) & (jnp.right_shift(lane, 3) == g_idx)
    el = jnp.where(emask, logits, NEG)
    m1 = jnp.max(el, axis=-1, keepdims=True)
    i1 = first_argmax(el, m1)
    el2 = jnp.where(lane == i1, NEG, el)
    m2 = jnp.max(el2, axis=-1, keepdims=True)
    i2 = first_argmax(el2, m2)
    zsum = jnp.sum(jnp.where(emask, jnp.exp(el - m1), 0.0), axis=-1, keepdims=True)
    p1 = 1.0 / zsum
    p2 = jnp.exp(m2 - m1) / zsum
    tot = p1 + p2
    w1 = g_w * (p1 / tot)
    w2 = g_w * (p2 / tot)
    rw_ref[0] = jnp.where(lane == 0, w1, jnp.where(lane == 1, w2, 0.0))

    tm = logits.shape[0]
    hot1 = lane == i1
    hot2 = lane == i2
    onehot = jnp.where(hot1, 1.0, jnp.where(hot2, 1.0, 0.0))
    r_i = lax.broadcasted_iota(jnp.int32, (tm, tm), 0)
    c_i = lax.broadcasted_iota(jnp.int32, (tm, tm), 1)
    before = jnp.where(c_i < r_i, 1.0, 0.0).astype(BF16)
    rank = cnt_ref[0:1, :] + jnp.dot(before, onehot.astype(BF16), preferred_element_type=F32)
    rank1 = jnp.sum(jnp.where(hot1, rank, 0.0), axis=-1, keepdims=True).astype(jnp.int32)
    rank2 = jnp.sum(jnp.where(hot2, rank, 0.0), axis=-1, keepdims=True).astype(jnp.int32)
    ri_ref[0] = jnp.where(lane == 0, i1, jnp.where(lane == 1, i2,
                          jnp.where(lane == 2, rank1, jnp.where(lane == 3, rank2, 0))))
    cnt_ref[...] = cnt_ref[...] + jnp.sum(onehot, axis=0, keepdims=True)


def _merge(x3, ya_t, yb_t, consts, *, tm):
    b, l, _ = x3.shape
    full = lambda a: pl.BlockSpec(a.shape, lambda bi, i: (0,) * a.ndim)
    tok = lambda w: pl.BlockSpec((1, tm, w), lambda bi, i: (bi, i, 0))
    yspec = pl.BlockSpec((1, ya_t.shape[1], tm), lambda bi, i: (bi, 0, i))
    return pl.pallas_call(
        _merge_kernel,
        grid=(b, l // tm),
        in_specs=[tok(D_MODEL), yspec, yspec] + [full(a) for a in consts],
        out_specs=[tok(D_MODEL),
                   pl.BlockSpec((tm * ROW_TILE, LANES), lambda bi, i: (bi * (l // tm) + i, 0)),
                   tok(LANES), tok(LANES),
                   pl.BlockSpec((8, LANES), lambda bi, i: (0, 0))],
        out_shape=[jax.ShapeDtypeStruct((b, l, D_MODEL), F32),
                   jax.ShapeDtypeStruct((b * l * ROW_TILE, LANES), F32),
                   jax.ShapeDtypeStruct((b, l, LANES), jnp.int32),
                   jax.ShapeDtypeStruct((b, l, LANES), F32),
                   jax.ShapeDtypeStruct((8, LANES), F32)],
        compiler_params=pltpu.CompilerParams(
            dimension_semantics=("arbitrary", "arbitrary"), vmem_limit_bytes=VMEM_LIMIT),
        name="merge",
    )(x3, ya_t, yb_t, *consts)


def _row_copy(src_hbm, row, buf, first, r, sem):
    return pltpu.make_async_copy(src_hbm.at[row], buf.at[pl.ds(first + r * ROW_PITCH, ROW_TILE)], sem)


def _row_gather(src_hbm, idx_ref, buf, first, sem, n_rows):
    def row(r, carry):
        _row_copy(src_hbm, idx_ref[0, 0, r], buf, first, r, sem).start()
        return carry
    lax.fori_loop(0, n_rows, row, 0, unroll=8)


def _row_gather_wait(src_hbm, buf, first, sem, n_rows):
    for _ in range(n_rows):
        _row_copy(src_hbm, 0, buf, first, 0, sem).wait()


def _expert_kernel(te_ref, tv_ref, idx_ref, idx_next_ref, t_hbm, wg_ref, wu_ref, wd_ref, y_ref,
                   xbuf, sem, *, tm, nt):
    del te_ref
    i = pl.program_id(0)
    slot = lax.rem(i, 2)
    nxt = jnp.minimum(i + 1, nt - 1)
    half = tm * ROW_PITCH

    @pl.when((i == 0) & (tv_ref[0] > 0))
    def _():
        _row_gather(t_hbm, idx_ref, xbuf, 0, sem.at[0], tm)

    @pl.when((i + 1 < nt) & (tv_ref[nxt] > 0))
    def _():
        _row_gather(t_hbm, idx_next_ref, xbuf, (1 - slot) * half, sem.at[1 - slot], tm)

    @pl.when(tv_ref[i] > 0)
    def _():
        _row_gather_wait(t_hbm, xbuf, slot * half, sem.at[slot], tm)
        x = _rows_from_tiles(xbuf, slot * half, tm, ROW_PITCH).astype(BF16)
        a = jnp.dot(x, wg_ref[0].astype(BF16), preferred_element_type=F32)
        u = jnp.dot(x, wu_ref[0].astype(BF16), preferred_element_type=F32)
        hm = (a * jax.nn.sigmoid(a)) * u
        _rows_to_tiles(y_ref, jnp.dot(hm.astype(BF16), wd_ref[0].astype(BF16), preferred_element_type=F32))

    @pl.when(tv_ref[i] == 0)
    def _():
        y_ref[...] = jnp.zeros_like(y_ref)


def _experts(tile_expert, tile_valid, src_idx, t_tiles, w_gate, w_up, w_down, *, tm):
    nt = src_idx.shape[0]
    wspec = lambda shape: pl.BlockSpec((1,) + shape, lambda i, te, tv: (te[i], 0, 0))
    grid_spec = pltpu.PrefetchScalarGridSpec(
        num_scalar_prefetch=2,
        grid=(nt,),
        in_specs=[pl.BlockSpec((1, 1, tm), lambda i, te, tv: (i, 0, 0), memory_space=pltpu.SMEM),
                  pl.BlockSpec((1, 1, tm), lambda i, te, tv: (jnp.minimum(i + 1, nt - 1), 0, 0),
                               memory_space=pltpu.SMEM),
                  pl.BlockSpec(memory_space=pl.ANY),
                  wspec((D_MODEL, D_EXPERT)), wspec((D_MODEL, D_EXPERT)), wspec((D_EXPERT, D_MODEL))],
        out_specs=pl.BlockSpec((tm * ROW_TILE, LANES), lambda i, te, tv: (i, 0)),
        scratch_shapes=[pltpu.VMEM((2 * tm * ROW_PITCH, LANES), F32), pltpu.SemaphoreType.DMA((2,))])
    return pl.pallas_call(
        functools.partial(_expert_kernel, tm=tm, nt=nt),
        grid_spec=grid_spec,
        out_shape=jax.ShapeDtypeStruct((nt * tm * ROW_TILE, LANES), F32),
        compiler_params=pltpu.CompilerParams(dimension_semantics=("arbitrary",), vmem_limit_bytes=VMEM_LIMIT),
        name="experts",
    )(tile_expert, tile_valid, src_idx, src_idx, t_tiles, w_gate, w_up, w_down)


def _combine_kernel(idx_ref, idx_next_ref, y_hbm, h1_ref, rw_ref, fn_ref, o_ref, ybuf, sem, *, tc, nt):
    i = pl.program_id(0)
    slot = lax.rem(i, 2)
    half = 2 * tc * ROW_PITCH

    @pl.when(i == 0)
    def _():
        _row_gather(y_hbm, idx_ref, ybuf, 0, sem.at[0], 2 * tc)

    @pl.when(i + 1 < nt)
    def _():
        _row_gather(y_hbm, idx_next_ref, ybuf, (1 - slot) * half, sem.at[1 - slot], 2 * tc)

    _row_gather_wait(y_hbm, ybuf, slot * half, sem.at[slot], 2 * tc)
    rw = rw_ref[...]
    y = (rw[:, 0:1] * _rows_from_tiles(ybuf, slot * half, tc, ROW_PITCH)
         + rw[:, 1:2] * _rows_from_tiles(ybuf, slot * half + tc * ROW_PITCH, tc, ROW_PITCH))
    o_ref[...] = _rms(h1_ref[...] + y, fn_ref[...])


def _combine(dst_idx, y_tiles, h2, rw2, final_norm, *, tc):
    n = h2.shape[0]
    nt = n // tc
    return pl.pallas_call(
        functools.partial(_combine_kernel, tc=tc, nt=nt),
        grid=(nt,),
        in_specs=[pl.BlockSpec((1, 1, 2 * tc), lambda i: (i, 0, 0), memory_space=pltpu.SMEM),
                  pl.BlockSpec((1, 1, 2 * tc), lambda i: (jnp.minimum(i + 1, nt - 1), 0, 0),
                               memory_space=pltpu.SMEM),
                  pl.BlockSpec(memory_space=pl.ANY),
                  pl.BlockSpec((tc, D_MODEL), lambda i: (i, 0)),
                  pl.BlockSpec((tc, LANES), lambda i: (i, 0)),
                  pl.BlockSpec((1, D_MODEL), lambda i: (0, 0))],
        out_specs=pl.BlockSpec((tc, D_MODEL), lambda i: (i, 0)),
        out_shape=jax.ShapeDtypeStruct((n, D_MODEL), F32),
        scratch_shapes=[pltpu.VMEM((2 * 2 * tc * ROW_PITCH, LANES), F32), pltpu.SemaphoreType.DMA((2,))],
        compiler_params=pltpu.CompilerParams(dimension_semantics=("arbitrary",), vmem_limit_bytes=VMEM_LIMIT),
        name="combine",
    )(dst_idx, dst_idx, y_tiles, h2, rw2, final_norm)


def _dispatch_plan(ri2, counts, *, tm):
    n = ri2.shape[0]
    nt = (2 * n) // tm + N_EXPERTS
    tiles_e = (counts + tm - 1) // tm
    tile_end = jnp.cumsum(tiles_e)
    tile_start = tile_end - tiles_e
    k = jnp.arange(nt, dtype=jnp.int32)
    tile_expert = jnp.minimum(jnp.sum((k[:, None] >= tile_end[None, :]).astype(jnp.int32), axis=1), N_EXPERTS - 1)
    tile_valid = jnp.clip(counts[tile_expert] - (k - tile_start[tile_expert]) * tm, 0, tm)
    tile_valid = jnp.where(k < tile_end[-1], tile_valid, 0).astype(jnp.int32)
    row_start = tile_start * tm
    e1, e2, r1, r2 = ri2[:, 0], ri2[:, 1], ri2[:, 2], ri2[:, 3]
    d1 = row_start[e1] + r1
    d2 = row_start[e2] + r2
    tok = jnp.arange(n, dtype=jnp.int32)
    src = jnp.zeros((nt * tm,), jnp.int32).at[d1].set(tok).at[d2].set(tok)
    return tile_expert.astype(jnp.int32), tile_valid, src.reshape(nt, 1, tm), d1, d2


def _pad_heads(w, width):
    k = w.shape[0]
    return jnp.pad(w.reshape(k, HEADS, width), ((0, 0), (0, 0), (0, HEAD_PAD - width))).reshape(k, HEADS * HEAD_PAD)


def _rope_tables(length):
    pos = jnp.arange(length, dtype=F32)
    inv = ROPE_THETA ** (-jnp.arange(0, MLA_ROPE, 2, dtype=F32) / MLA_ROPE)
    ang = pos[:, None] * inv[None, :]
    cos, sin = jnp.cos(ang), jnp.sin(ang)
    one = jnp.ones((length, ROPE_LO), F32)
    zero_lo = jnp.zeros((length, ROPE_LO), F32)
    zero_hi = jnp.zeros((length, LANES - ROPE_LO - MLA_ROPE), F32)
    return (jnp.concatenate([one, cos, cos, zero_hi], axis=1),
            jnp.concatenate([zero_lo, -sin, sin, zero_hi], axis=1))


def _decay_columns():
    eq = np.zeros((N_SPLIT * LANES, HEADS * HEAD_PAD), np.float32)
    ek = np.zeros_like(eq)
    oq = np.zeros((1, HEADS * HEAD_PAD), np.float32)
    ok = np.zeros_like(oq)
    for h in range(HEADS):
        for p in range(N_SPLIT):
            eq[p * LANES + h, h * HEAD_PAD + FOX_AUG + p] = 1.0
            ek[p * LANES + h, h * HEAD_PAD + FOX_AUG + N_SPLIT + p] = -1.0
            oq[0, h * HEAD_PAD + FOX_AUG + N_SPLIT + p] = 1.0
            ok[0, h * HEAD_PAD + FOX_AUG + p] = 1.0
    return jnp.asarray(eq, BF16), jnp.asarray(ek, BF16), jnp.asarray(oq), jnp.asarray(ok)


def kernel(x, meta, attn_norm, w_in, b_forget, q_a_norm, w_q_up, kv_a_norm, w_kv_up, w_mla_out, w_fox_out, w_out, ffn_norm, w_group_router, b_group_router, w_expert_router, b_expert_router, w_gate, w_up, w_down, final_norm):
    b, seq, d = x.shape
    t_attn = 512
    row = lambda v: v.reshape(1, -1).astype(F32)

    w = w_in[0]
    offs = np.cumsum([0, Q_LORA, KV_LORA, MLA_ROPE, HEADS * FOX_HD, HEADS * FOX_HD, HEADS * FOX_HD,
                      HEADS, D_MODEL, D_MODEL])
    seg = lambda j: w[:, offs[j]:offs[j + 1]]
    zcol = lambda n: jnp.zeros((d, n), F32)
    w_a = jnp.concatenate(
        [seg(0), seg(1),
         zcol(ROPE_LO), seg(2), zcol(LANES - ROPE_LO - MLA_ROPE),
         seg(6), zcol(LANES - HEADS),
         _pad_heads(seg(3), FOX_HD), _pad_heads(seg(4), FOX_HD)], axis=1).astype(BF16)
    w_fvt = seg(5).T.astype(BF16)
    w_gates = jnp.concatenate([seg(7), seg(8)], axis=1).astype(BF16)
    w_q = _pad_heads(w_q_up[0], MLA_NOPE + MLA_ROPE).astype(BF16)
    kv3 = w_kv_up[0].reshape(KV_LORA, HEADS, MLA_NOPE + MLA_V)
    w_k = _pad_heads(kv3[:, :, :MLA_NOPE].reshape(KV_LORA, HEADS * MLA_NOPE), MLA_NOPE).astype(BF16)
    w_vt = kv3[:, :, MLA_NOPE:].reshape(KV_LORA, HEADS * MLA_V).T.astype(BF16)
    bf128 = jnp.pad(row(b_forget[0]), ((0, 0), (0, LANES - HEADS)))
    eq, ek, oq, ok = _decay_columns()
    cos, sin = _rope_tables(N_META + seq)
    consts = [row(attn_norm[0]), w_a, row(q_a_norm[0]), w_q, row(kv_a_norm[0]), w_k, w_vt, w_fvt, bf128,
              eq, ek, oq, ok]

    meta_pad = jnp.pad(meta.astype(F32), ((0, META_PAD - N_META), (0, 0)))[None]
    pad_tab = lambda tab: jnp.pad(tab[:N_META], ((0, META_PAD - N_META), (0, 0)))
    _, km_meta, vm_meta, _, kf_meta, vf_meta, c_meta = _projections(
        meta_pad, pad_tab(cos), pad_tab(sin), jnp.zeros((1, LANES), F32), consts, tm=META_PAD)
    c0 = c_meta[0, N_META - 1:N_META, :]

    qm, km, vm, qf, kf, vf, _ = _projections(x, cos[N_META:], sin[N_META:], c0, consts, tm=t_attn)
    ya_t = _attention(qm, km, vm, km_meta, vm_meta, t=t_attn, g=ATTN_HEADS_PER_STEP)
    yb_t = _attention(qf, kf, vf, kf_meta, vf_meta, t=t_attn, g=ATTN_HEADS_PER_STEP)

    w_r = jnp.concatenate([w_expert_router[0], w_group_router[0],
                           jnp.zeros((d, LANES - N_EXPERTS - N_GROUPS), F32)], axis=1)
    w_rh = w_r.astype(BF16)
    w_rl = (w_r - w_rh.astype(F32)).astype(BF16)
    b_r = jnp.pad(jnp.concatenate([row(b_expert_router[0]), row(b_group_router[0])], axis=1),
                  ((0, 0), (0, LANES - N_EXPERTS - N_GROUPS)))
    h1, t_tiles, ri, rw, cnt = _merge(
        x, ya_t, yb_t,
        [row(attn_norm[0]), w_gates, w_mla_out[0].astype(BF16), w_fox_out[0].astype(BF16),
         w_out[0].astype(BF16), row(ffn_norm[0]), w_rh, w_rl, b_r], tm=512)

    n = b * seq
    tile_expert, tile_valid, src_idx, d1, d2 = _dispatch_plan(
        ri.reshape(n, LANES), cnt[0, :N_EXPERTS].astype(jnp.int32), tm=EXPERT_ROWS)
    y_tiles = _experts(tile_expert, tile_valid, src_idx, t_tiles.reshape(n, ROW_TILE, LANES),
                       w_gate[0].reshape(N_EXPERTS, d, D_EXPERT), w_up[0].reshape(N_EXPERTS, d, D_EXPERT),
                       w_down[0].reshape(N_EXPERTS, D_EXPERT, d), tm=EXPERT_ROWS)
    tc = COMBINE_TOKENS
    dst_idx = jnp.concatenate([d1.reshape(n // tc, 1, tc), d2.reshape(n // tc, 1, tc)], axis=2)
    out = _combine(dst_idx, y_tiles.reshape(-1, ROW_TILE, LANES), h1.reshape(n, d), rw.reshape(n, LANES),
                   row(final_norm), tc=tc)
    return out.reshape(b, seq, d)
```

```python
import functools

import numpy as np
import jax
import jax.numpy as jnp
from jax import lax
from jax.experimental import pallas as pl
from jax.experimental.pallas import tpu as pltpu

F32 = jnp.float32
BF16 = jnp.bfloat16

D_MODEL = 1024
N_META = 16
HEADS = 8
MLA_NOPE, MLA_ROPE, MLA_V = 64, 32, 64
Q_LORA, KV_LORA = 512, 256
ROPE_THETA = 10000.0
FOX_HD = 64
N_GROUPS, EXPERTS_PER_GROUP, D_EXPERT = 4, 8, 256
N_EXPERTS = N_GROUPS * EXPERTS_PER_GROUP
EPS = 1e-6
NEG = -1e30
LOG2E = 1.4426950408889634

LANES = 128
HEAD_PAD = LANES
META_PAD = LANES
ROPE_LO = MLA_NOPE
ROPE_HALF = MLA_ROPE // 2
FOX_AUG = FOX_HD
N_SPLIT = 3

VMEM_LIMIT = 56 * 1024 * 1024
ATTN_QUERIES = 1024
ATTN_HEADS_PER_STEP = 8
SKEW = 2
ACC_PAD = 16
ROW_TILE = D_MODEL // LANES
ROW_PITCH = ROW_TILE + 1
IDX_FIELDS = 4
EXPERT_ROWS = 256
COMBINE_TOKENS = 256

_NT = (((1,), (1,)), ((), ()))
_TN = (((0,), (0,)), ((), ()))


def _rms(x, g):
    return x * lax.rsqrt(jnp.mean(x * x, axis=-1, keepdims=True) + EPS) * g


def _rows_to_tiles(ref2d, v, pitch=ROW_TILE):
    n = v.shape[0]
    for c in range(ROW_TILE):
        ref2d[pl.ds(c, n, stride=pitch), :] = v[:, c * LANES:(c + 1) * LANES]
    for c in range(ROW_TILE, pitch):
        ref2d[pl.ds(c, n, stride=pitch), :] = jnp.zeros((n, LANES), v.dtype)


def _rows_from_tiles(ref2d, first, n, pitch):
    return jnp.concatenate([ref2d[pl.ds(first + c, n, stride=pitch), :] for c in range(ROW_TILE)], axis=1)


def _split3(v):
    hi = v.astype(BF16)
    r = v - hi.astype(F32)
    mid = r.astype(BF16)
    lo = (r - mid.astype(F32)).astype(BF16)
    return hi, mid, lo


def _rope_tile(v, c, s):
    lane = lax.broadcasted_iota(jnp.int32, v.shape, 1)
    partner = jnp.where(lane < ROPE_LO + ROPE_HALF,
                        pltpu.roll(v, LANES - ROPE_HALF, 1), pltpu.roll(v, ROPE_HALF, 1))
    return v * c + partner * s


def _proj_kernel(x_ref, cos_ref, sin_ref, c0_ref, an_ref, wa_ref, qn_ref, wq_ref, kvn_ref, wk_ref,
                 wvt_ref, wfvt_ref, bf_ref, eq_ref, ek_ref, oq_ref, ok_ref,
                 qm_ref, km_ref, vm_ref, qf_ref, kf_ref, vf_ref, cl_ref, carry_ref, *, tm):
    i = pl.program_id(1)

    @pl.when(i == 0)
    def _():
        carry_ref[...] = jnp.broadcast_to(c0_ref[...], carry_ref.shape)

    ub = _rms(x_ref[0], an_ref[...]).astype(BF16)
    pa = jnp.dot(ub, wa_ref[...], preferred_element_type=F32)
    c_q = pa[:, 0:Q_LORA]
    c_kv = pa[:, Q_LORA:Q_LORA + KV_LORA]
    o = Q_LORA + KV_LORA
    kr = pa[:, o:o + LANES]
    fl = pa[:, o + LANES:o + 2 * LANES]
    o += 2 * LANES
    fq = pa[:, o:o + HEADS * HEAD_PAD]
    fk = pa[:, o + HEADS * HEAD_PAD:o + 2 * HEADS * HEAD_PAD]

    cos = cos_ref[...]
    sin = sin_ref[...]

    cqn = _rms(c_q, qn_ref[...]).astype(BF16)
    q = jnp.dot(cqn, wq_ref[...], preferred_element_type=F32)
    ckvn = _rms(c_kv, kvn_ref[...]).astype(BF16)
    kb = jnp.dot(ckvn, wk_ref[...], preferred_element_type=F32)
    krr = _rope_tile(kr, cos, sin)
    q_scale = (MLA_NOPE + MLA_ROPE) ** -0.5 * LOG2E
    for h in range(HEADS):
        sl = slice(h * HEAD_PAD, (h + 1) * HEAD_PAD)
        qm_ref[0, h] = (_rope_tile(q[:, sl], cos, sin) * q_scale).astype(BF16)
        km_ref[0, h] = (kb[:, sl] + krr).astype(BF16)
    vt = lax.dot_general(wvt_ref[...], ckvn, _NT, preferred_element_type=F32)
    for h in range(HEADS):
        vm_ref[0, h, 0] = vt[h * MLA_V:(h + 1) * MLA_V, :].astype(BF16)

    z = fl + bf_ref[...]
    logf = jnp.minimum(z, 0.0) - jnp.log1p(jnp.exp(-jnp.abs(z)))
    r_i = lax.broadcasted_iota(jnp.int32, (tm, tm), 0)
    c_i = lax.broadcasted_iota(jnp.int32, (tm, tm), 1)
    tri = jnp.where(c_i <= r_i, 1.0, 0.0).astype(BF16)
    cs = sum(jnp.dot(tri, piece, preferred_element_type=F32) for piece in _split3(logf))
    c = carry_ref[0:1, :] + cs
    carry_ref[...] = jnp.broadcast_to(c[tm - 1:tm, :], carry_ref.shape)
    cl_ref[0] = c
    cat = jnp.concatenate(_split3(c * LOG2E), axis=1)
    augq = jnp.dot(cat, eq_ref[...], preferred_element_type=F32)
    augk = jnp.dot(cat, ek_ref[...], preferred_element_type=F32)
    qfull = fq * (FOX_HD ** -0.5 * LOG2E) + augq + oq_ref[...]
    kfull = fk + augk + ok_ref[...]
    for h in range(HEADS):
        sl = slice(h * HEAD_PAD, (h + 1) * HEAD_PAD)
        qf_ref[0, h] = qfull[:, sl].astype(BF16)
        kf_ref[0, h] = kfull[:, sl].astype(BF16)
    vft = lax.dot_general(wfvt_ref[...], ub, _NT, preferred_element_type=F32)
    for h in range(HEADS):
        vf_ref[0, h, 0] = vft[h * FOX_HD:(h + 1) * FOX_HD, :].astype(BF16)


def _projections(x3, cos, sin, c0, consts, *, tm):
    b, l, _ = x3.shape
    nt = l // tm
    full = lambda a: pl.BlockSpec(a.shape, lambda bi, i: (0,) * a.ndim)
    qk_shape = jax.ShapeDtypeStruct((b, HEADS, l, HEAD_PAD), BF16)
    vt_shape = jax.ShapeDtypeStruct((b, HEADS, nt, MLA_V, tm), BF16)
    qk_spec = pl.BlockSpec((1, HEADS, tm, HEAD_PAD), lambda bi, i: (bi, 0, i, 0))
    vt_spec = pl.BlockSpec((1, HEADS, 1, MLA_V, tm), lambda bi, i: (bi, 0, i, 0, 0))
    return pl.pallas_call(
        functools.partial(_proj_kernel, tm=tm),
        grid=(b, nt),
        in_specs=[pl.BlockSpec((1, tm, D_MODEL), lambda bi, i: (bi, i, 0)),
                  pl.BlockSpec((tm, LANES), lambda bi, i: (i, 0)),
                  pl.BlockSpec((tm, LANES), lambda bi, i: (i, 0)),
                  full(c0)] + [full(a) for a in consts],
        out_specs=[qk_spec, qk_spec, vt_spec, qk_spec, qk_spec, vt_spec,
                   pl.BlockSpec((1, tm, LANES), lambda bi, i: (bi, i, 0))],
        out_shape=[qk_shape, qk_shape, vt_shape, qk_shape, qk_shape, vt_shape,
                   jax.ShapeDtypeStruct((b, l, LANES), F32)],
        scratch_shapes=[pltpu.VMEM((8, LANES), F32)],
        compiler_params=pltpu.CompilerParams(
            dimension_semantics=("arbitrary", "arbitrary"), vmem_limit_bytes=VMEM_LIMIT),
        name="proj",
    )(x3, cos, sin, c0, *consts)


def _attn_kernel(q_ref, k_ref, vt_ref, km_ref, vtm_ref, o_ref, m_ref, acc_ref, *, t, tk, g):
    i = pl.program_id(2)
    dv = vt_ref.shape[3]

    def ones_rows(n):
        return jnp.where(lax.broadcasted_iota(jnp.int32, (ACC_PAD, n), 0) == 0, 1.0, 0.0).astype(BF16)

    def update(hd, s, vt_blk, first=False):
        m_cur = jnp.max(s, axis=0, keepdims=True)
        if first:
            m_new = m_cur
        else:
            m_prev = m_ref[hd]
            m_new = jnp.maximum(m_prev, m_cur)
            alpha = jnp.exp2(m_prev - m_new)
        p = jnp.exp2(s - m_new).astype(BF16)
        vt_aug = jnp.concatenate([vt_blk, ones_rows(p.shape[0])], axis=0)
        pv = jnp.dot(vt_aug, p, preferred_element_type=F32)
        acc_ref[hd] = pv if first else alpha * acc_ref[hd] + pv
        m_ref[hd] = m_new

    def scores(hd, k):
        return lax.dot_general(k, q_ref[0, hd], _NT, preferred_element_type=F32)

    def keys(j):
        off = pl.multiple_of(j * tk, tk)
        return lambda hd: k_ref[0, hd, pl.ds(off, tk), :]

    def sweep(keys_of, vt_of, mask, first=False):
        pending = [scores(hd, keys_of(hd)) for hd in range(SKEW)]
        for hd in range(g):
            if hd + SKEW < g:
                pending.append(scores(hd + SKEW, keys_of(hd + SKEW)))
            s = pending.pop(0)
            if mask is not None:
                s = jnp.where(mask, s, NEG)
            update(hd, s, vt_of(hd), first=first)

    row = lax.broadcasted_iota(jnp.int32, (META_PAD, t), 0)
    sweep(lambda hd: km_ref[0, hd], lambda hd: vtm_ref[0, hd, 0], row < N_META, first=True)

    def body(j, carry):
        sweep(keys(j), lambda hd: vt_ref[0, hd, j], None)
        return carry

    per = t // tk
    lax.fori_loop(0, i * per, body, 0)

    key_row = lax.broadcasted_iota(jnp.int32, (tk, t), 0)
    query = lax.broadcasted_iota(jnp.int32, (tk, t), 1)
    for d in range(per):
        jd = i * per + d
        sweep(keys(jd), lambda hd, jd=jd: vt_ref[0, hd, jd], key_row + d * tk <= query)

    for hd in range(g):
        a = acc_ref[hd]
        o_ref[0, hd * dv:(hd + 1) * dv, :] = (a[:dv] / a[dv:dv + 1]).astype(o_ref.dtype)


def _attention(q, k, vt, k_meta, vt_meta, *, t, tk, g):
    b, h, l, _ = q.shape
    nq = l // t
    dv = vt.shape[3]
    return pl.pallas_call(
        functools.partial(_attn_kernel, t=t, tk=tk, g=g),
        grid=(b, h // g, nq),
        in_specs=[pl.BlockSpec((1, g, t, HEAD_PAD), lambda bi, hi, i: (bi, hi, i, 0)),
                  pl.BlockSpec((1, g, l, HEAD_PAD), lambda bi, hi, i: (bi, hi, 0, 0),
                               pipeline_mode=pl.Buffered(1)),
                  pl.BlockSpec((1, g, l // tk, dv, tk), lambda bi, hi, i: (bi, hi, 0, 0, 0),
                               pipeline_mode=pl.Buffered(1)),
                  pl.BlockSpec((1, g, META_PAD, HEAD_PAD), lambda bi, hi, i: (0, hi, 0, 0)),
                  pl.BlockSpec((1, g, 1, dv, META_PAD), lambda bi, hi, i: (0, hi, 0, 0, 0))],
        out_specs=pl.BlockSpec((1, g * dv, t), lambda bi, hi, i: (bi, hi, i)),
        out_shape=jax.ShapeDtypeStruct((b, h * dv, l), BF16),
        scratch_shapes=[pltpu.VMEM((g, 1, t), F32), pltpu.VMEM((g, dv + ACC_PAD, t), F32)],
        compiler_params=pltpu.CompilerParams(
            dimension_semantics=("arbitrary", "arbitrary", "arbitrary"), vmem_limit_bytes=VMEM_LIMIT),
        name="attn",
    )(q, k, vt, k_meta, vt_meta)


def _merge_kernel(x_ref, ya_ref, yb_ref, an_ref, wg_ref, wmo_ref, wfo_ref, wo_ref, fn_ref,
                  wrh_ref, wrl_ref, br_ref, h1_ref, t_ref, ri_ref, rw_ref, cnt_ref):
    @pl.when((pl.program_id(0) == 0) & (pl.program_id(1) == 0))
    def _():
        cnt_ref[...] = jnp.zeros_like(cnt_ref)

    x = x_ref[0]
    ub = _rms(x, an_ref[...]).astype(BF16)
    g = jnp.dot(ub, wg_ref[...], preferred_element_type=F32)
    ya = lax.dot_general(ya_ref[0], wmo_ref[...], _TN, preferred_element_type=F32)
    yb = lax.dot_general(yb_ref[0], wfo_ref[...], _TN, preferred_element_type=F32)
    merged = jax.nn.sigmoid(g[:, :D_MODEL]) * ya + jax.nn.sigmoid(g[:, D_MODEL:]) * yb
    h1 = x + jnp.dot(merged.astype(BF16), wo_ref[...], preferred_element_type=F32)
    h1_ref[0] = h1
    t = _rms(h1, fn_ref[...])
    _rows_to_tiles(t_ref, t, ROW_PITCH)

    th = t.astype(BF16)
    tl = (t - th.astype(F32)).astype(BF16)
    logits = (jnp.dot(th, wrh_ref[...], preferred_element_type=F32)
              + jnp.dot(tl, wrh_ref[...], preferred_element_type=F32)
              + jnp.dot(th, wrl_ref[...], preferred_element_type=F32)) + br_ref[...]
    lane = lax.broadcasted_iota(jnp.int32, logits.shape, 1)
    big = jnp.int32(LANES)

    def first_argmax(v, vmax):
        return jnp.min(jnp.where(v == vmax, lane, big), axis=-1, keepdims=True)

    gmask = (lane >= N_EXPERTS) & (lane < N_EXPERTS + N_GROUPS)
    gl = jnp.where(gmask, logits, NEG)
    gmax = jnp.max(gl, axis=-1, keepdims=True)
    g_w = 1.0 / jnp.sum(jnp.where(gmask, jnp.exp(gl - gmax), 0.0), axis=-1, keepdims=True)
    g_idx = first_argmax(gl, gmax) - N_EXPERTS
    emask = (lane < N_EXPERTS) & (jnp.right_shift(lane, 3) == g_idx)
    el = jnp.where(emask, logits, NEG)
    m1 = jnp.max(el, axis=-1, keepdims=True)
    i1 = first_argmax(el, m1)
    el2 = jnp.where(lane == i1, NEG, el)
    m2 = jnp.max(el2, axis=-1, keepdims=True)
    i2 = first_argmax(el2, m2)
    zsum = jnp.sum(jnp.where(emask, jnp.exp(el - m1), 0.0), axis=-1, keepdims=True)
    p1 = 1.0 / zsum
    p2 = jnp.exp(m2 - m1) / zsum
    tot = p1 + p2
    w1 = g_w * (p1 / tot)
    w2 = g_w * (p2 / tot)
    rw_ref[0] = jnp.where(lane == 0, w1, jnp.where(lane == 1, w2, 0.0))

    tm = logits.shape[0]
    hot1 = lane == i1
    hot2 = lane == i2
    onehot = jnp.where(hot1, 1.0, jnp.where(hot2, 1.0, 0.0))
    r_i = lax.broadcasted_iota(jnp.int32, (tm, tm), 0)
    c_i = lax.broadcasted_iota(jnp.int32, (tm, tm), 1)
    before = jnp.where(c_i < r_i, 1.0, 0.0).astype(BF16)
    rank = cnt_ref[0:1, :] + jnp.dot(before, onehot.astype(BF16), preferred_element_type=F32)
    rank1 = jnp.sum(jnp.where(hot1, rank, 0.0), axis=-1, keepdims=True).astype(jnp.int32)
    rank2 = jnp.sum(jnp.where(hot2, rank, 0.0), axis=-1, keepdims=True).astype(jnp.int32)
    fields = jnp.where(lane == 0, i1, jnp.where(lane == 1, i2,
                       jnp.where(lane == 2, rank1, jnp.where(lane == 3, rank2, 0))))
    ri_ref[0] = fields.T[0:IDX_FIELDS, :]
    cnt_ref[...] = cnt_ref[...] + jnp.sum(onehot, axis=0, keepdims=True)


def _merge(x3, ya_t, yb_t, consts, *, tm):
    b, l, _ = x3.shape
    full = lambda a: pl.BlockSpec(a.shape, lambda bi, i: (0,) * a.ndim)
    tok = lambda w: pl.BlockSpec((1, tm, w), lambda bi, i: (bi, i, 0))
    yspec = pl.BlockSpec((1, ya_t.shape[1], tm), lambda bi, i: (bi, 0, i))
    return pl.pallas_call(
        _merge_kernel,
        grid=(b, l // tm),
        in_specs=[tok(D_MODEL), yspec, yspec] + [full(a) for a in consts],
        out_specs=[tok(D_MODEL),
                   pl.BlockSpec((tm * ROW_PITCH, LANES), lambda bi, i: (bi * (l // tm) + i, 0)),
                   pl.BlockSpec((1, IDX_FIELDS, tm), lambda bi, i: (bi * (l // tm) + i, 0, 0)), tok(LANES),
                   pl.BlockSpec((8, LANES), lambda bi, i: (0, 0))],
        out_shape=[jax.ShapeDtypeStruct((b, l, D_MODEL), F32),
                   jax.ShapeDtypeStruct((b * l * ROW_PITCH, LANES), F32),
                   jax.ShapeDtypeStruct((b * (l // tm), IDX_FIELDS, tm), jnp.int32),
                   jax.ShapeDtypeStruct((b, l, LANES), F32),
                   jax.ShapeDtypeStruct((8, LANES), F32)],
        compiler_params=pltpu.CompilerParams(
            dimension_semantics=("arbitrary", "arbitrary"), vmem_limit_bytes=VMEM_LIMIT),
        name="merge",
    )(x3, ya_t, yb_t, *consts)


def _dst_rows_kernel(rs_ref, idx_ref, d_ref):
    for k in range(2):
        e = idx_ref[:, k, :]
        start = jnp.zeros_like(e)
        for j in range(N_EXPERTS):
            start = jnp.where(e == j, rs_ref[j], start)
        d_ref[:, k, :] = start + idx_ref[:, 2 + k, :]


def _dst_rows(row_start, idx):
    nt, _, tm = idx.shape
    return pl.pallas_call(
        _dst_rows_kernel,
        grid_spec=pltpu.PrefetchScalarGridSpec(
            num_scalar_prefetch=1, grid=(1,),
            in_specs=[pl.BlockSpec(idx.shape, lambda i, rs: (0, 0, 0))],
            out_specs=pl.BlockSpec((nt, 2, tm), lambda i, rs: (0, 0, 0))),
        out_shape=jax.ShapeDtypeStruct((nt, 2, tm), jnp.int32),
        name="dst_rows",
    )(row_start, idx)


def _dispatch_kernel(rs_ref, cnt_ref, pad_ref, used_ref, d_ref, t_ref, xs_hbm, zero_ref, sem, zsem, *,
                     tm, te, nt):
    def row_tile(row):
        return xs_hbm.at[pl.ds(row * ROW_PITCH, ROW_PITCH)]

    def zero_row(row):
        return pltpu.make_async_copy(zero_ref.at[pl.ds(0, ROW_PITCH)], row_tile(row), zsem)

    def zero_tile(k):
        return pltpu.make_async_copy(zero_ref, xs_hbm.at[pl.ds(k * (te * ROW_PITCH), te * ROW_PITCH)], zsem)

    @pl.when(pl.program_id(0) == 0)
    def _():
        zero_ref[...] = jnp.zeros_like(zero_ref)
        for e in range(N_EXPERTS):
            def fill(p, carry, e=e):
                zero_row(rs_ref[e] + p).start()
                return carry
            lax.fori_loop(cnt_ref[e], pad_ref[e], fill, 0)

        def fill_tile(k, carry):
            zero_tile(k).start()
            return carry
        lax.fori_loop(used_ref[0], nt, fill_tile, 0)
        for e in range(N_EXPERTS):
            def drain(p, carry):
                zero_row(0).wait()
                return carry
            lax.fori_loop(cnt_ref[e], pad_ref[e], drain, 0)

        def drain_tile(k, carry):
            zero_tile(0).wait()
            return carry
        lax.fori_loop(used_ref[0], nt, drain_tile, 0)

    def copy(r, k):
        return pltpu.make_async_copy(t_ref.at[pl.ds(r * ROW_PITCH, ROW_PITCH)], row_tile(d_ref[0, k, r]), sem)

    def row(r, carry):
        copy(r, 0).start()
        copy(r, 1).start(priority=1)
        return carry
    lax.fori_loop(0, tm, row, 0, unroll=8)
    for _ in range(2 * tm):
        copy(0, 0).wait()


def _dispatch(row_start, counts, padded, tiles_used, dst, t_tiles2d, n_tiles, *, tm, te):
    grid_spec = pltpu.PrefetchScalarGridSpec(
        num_scalar_prefetch=4,
        grid=(dst.shape[0],),
        in_specs=[pl.BlockSpec((1, 2, tm), lambda i, *_: (i, 0, 0), memory_space=pltpu.SMEM),
                  pl.BlockSpec((tm * ROW_PITCH, LANES), lambda i, *_: (i, 0))],
        out_specs=pl.BlockSpec(memory_space=pl.ANY),
        scratch_shapes=[pltpu.VMEM((te * ROW_PITCH, LANES), F32), pltpu.SemaphoreType.DMA(()),
                        pltpu.SemaphoreType.DMA(())])
    return pl.pallas_call(
        functools.partial(_dispatch_kernel, tm=tm, te=te, nt=n_tiles),
        grid_spec=grid_spec,
        out_shape=jax.ShapeDtypeStruct((n_tiles * te * ROW_PITCH, LANES), F32),
        compiler_params=pltpu.CompilerParams(dimension_semantics=("arbitrary",), vmem_limit_bytes=VMEM_LIMIT),
        name="dispatch",
    )(row_start, counts, padded, tiles_used, dst, t_tiles2d)


def _expert_kernel(te_ref, tv_ref, x_ref, wg_ref, wu_ref, wd_ref, y_ref, *, tm):
    del te_ref
    i = pl.program_id(0)

    @pl.when(tv_ref[i] > 0)
    def _():
        x = _rows_from_tiles(x_ref, 0, tm, ROW_PITCH).astype(BF16)
        a = jnp.dot(x, wg_ref[0].astype(BF16), preferred_element_type=F32)
        u = jnp.dot(x, wu_ref[0].astype(BF16), preferred_element_type=F32)
        hm = (a * jax.nn.sigmoid(a)) * u
        _rows_to_tiles(y_ref, jnp.dot(hm.astype(BF16), wd_ref[0].astype(BF16), preferred_element_type=F32))

    @pl.when(tv_ref[i] == 0)
    def _():
        y_ref[...] = jnp.zeros_like(y_ref)


def _experts(tile_expert, tile_valid, xs, w_gate, w_up, w_down, *, tm):
    nt = tile_expert.shape[0]
    used = lambda i, tv: jnp.where(tv[i] > 0, i, 0)
    wspec = lambda shape: pl.BlockSpec((1,) + shape, lambda i, te, tv: (te[i], 0, 0))
    grid_spec = pltpu.PrefetchScalarGridSpec(
        num_scalar_prefetch=2,
        grid=(nt,),
        in_specs=[pl.BlockSpec((tm * ROW_PITCH, LANES), lambda i, te, tv: (used(i, tv), 0)),
                  wspec((D_MODEL, D_EXPERT)), wspec((D_MODEL, D_EXPERT)), wspec((D_EXPERT, D_MODEL))],
        out_specs=pl.BlockSpec((tm * ROW_TILE, LANES), lambda i, te, tv: (i, 0)))
    return pl.pallas_call(
        functools.partial(_expert_kernel, tm=tm),
        grid_spec=grid_spec,
        out_shape=jax.ShapeDtypeStruct((nt * tm * ROW_TILE, LANES), F32),
        compiler_params=pltpu.CompilerParams(dimension_semantics=("arbitrary",), vmem_limit_bytes=VMEM_LIMIT),
        name="experts",
    )(tile_expert, tile_valid, xs, w_gate, w_up, w_down)


def _combine_kernel(d_ref, d_next_ref, y_hbm, h1_ref, rw_ref, fn_ref, o_ref, ybuf, sem, *, tc, nt):
    i = pl.program_id(0)
    slot = lax.rem(i, 2)
    half = 2 * tc * ROW_PITCH

    def copy(dr, first, s, r, k):
        return pltpu.make_async_copy(y_hbm.at[dr[0, k, r]],
                                     ybuf.at[pl.ds(first + (k * tc + r) * ROW_PITCH, ROW_TILE)], sem.at[s])

    def gather(dr, first, s):
        def row(r, carry):
            copy(dr, first, s, r, 0).start()
            copy(dr, first, s, r, 1).start(priority=1)
            return carry
        lax.fori_loop(0, tc, row, 0, unroll=8)

    @pl.when(i == 0)
    def _():
        gather(d_ref, 0, 0)

    @pl.when(i + 1 < nt)
    def _():
        gather(d_next_ref, (1 - slot) * half, 1 - slot)

    for _ in range(2 * tc):
        copy(d_ref, slot * half, slot, 0, 0).wait()
    rw = rw_ref[...]
    y = (rw[:, 0:1] * _rows_from_tiles(ybuf, slot * half, tc, ROW_PITCH)
         + rw[:, 1:2] * _rows_from_tiles(ybuf, slot * half + tc * ROW_PITCH, tc, ROW_PITCH))
    o_ref[...] = _rms(h1_ref[...] + y, fn_ref[...])


def _combine(dst, y_tiles, h2, rw2, final_norm, *, tc):
    n = h2.shape[0]
    nt = n // tc
    per = dst.shape[2] // tc
    blk = lambda i: (i // per, 0, lax.rem(i, per))
    return pl.pallas_call(
        functools.partial(_combine_kernel, tc=tc, nt=nt),
        grid=(nt,),
        in_specs=[pl.BlockSpec((1, 2, tc), lambda i: blk(i), memory_space=pltpu.SMEM),
                  pl.BlockSpec((1, 2, tc), lambda i: blk(jnp.minimum(i + 1, nt - 1)), memory_space=pltpu.SMEM),
                  pl.BlockSpec(memory_space=pl.ANY),
                  pl.BlockSpec((tc, D_MODEL), lambda i: (i, 0)),
                  pl.BlockSpec((tc, LANES), lambda i: (i, 0)),
                  pl.BlockSpec((1, D_MODEL), lambda i: (0, 0))],
        out_specs=pl.BlockSpec((tc, D_MODEL), lambda i: (i, 0)),
        out_shape=jax.ShapeDtypeStruct((n, D_MODEL), F32),
        scratch_shapes=[pltpu.VMEM((2 * 2 * tc * ROW_PITCH, LANES), F32), pltpu.SemaphoreType.DMA((2,))],
        compiler_params=pltpu.CompilerParams(dimension_semantics=("arbitrary",), vmem_limit_bytes=VMEM_LIMIT),
        name="combine",
    )(dst, dst, y_tiles, h2, rw2, final_norm)


def _tile_plan(counts, n, *, tm):
    nt = (2 * n) // tm + N_EXPERTS
    tiles_e = (counts + tm - 1) // tm
    tile_end = jnp.cumsum(tiles_e)
    tile_start = tile_end - tiles_e
    k = jnp.arange(nt, dtype=jnp.int32)
    tile_expert = jnp.minimum(jnp.sum((k[:, None] >= tile_end[None, :]).astype(jnp.int32), axis=1), N_EXPERTS - 1)
    onehot = (tile_expert[:, None] == jnp.arange(N_EXPERTS)[None, :]).astype(jnp.int32)
    left = jnp.sum(onehot * (counts[None, :] - (k[:, None] - tile_start[None, :]) * tm), axis=1)
    tile_valid = jnp.where(k < tile_end[-1], jnp.clip(left, 0, tm), 0).astype(jnp.int32)
    return ((tile_start * tm).astype(jnp.int32), (tiles_e * tm).astype(jnp.int32),
            tile_end[-1:].astype(jnp.int32), tile_expert.astype(jnp.int32), tile_valid, nt)


def _pad_heads(w, width):
    k = w.shape[0]
    return jnp.pad(w.reshape(k, HEADS, width), ((0, 0), (0, 0), (0, HEAD_PAD - width))).reshape(k, HEADS * HEAD_PAD)


def _rope_tables(length):
    pos = jnp.arange(length, dtype=F32)
    inv = ROPE_THETA ** (-jnp.arange(0, MLA_ROPE, 2, dtype=F32) / MLA_ROPE)
    ang = pos[:, None] * inv[None, :]
    cos, sin = jnp.cos(ang), jnp.sin(ang)
    one = jnp.ones((length, ROPE_LO), F32)
    zero_lo = jnp.zeros((length, ROPE_LO), F32)
    zero_hi = jnp.zeros((length, LANES - ROPE_LO - MLA_ROPE), F32)
    return (jnp.concatenate([one, cos, cos, zero_hi], axis=1),
            jnp.concatenate([zero_lo, -sin, sin, zero_hi], axis=1))


def _decay_columns():
    eq = np.zeros((N_SPLIT * LANES, HEADS * HEAD_PAD), np.float32)
    ek = np.zeros_like(eq)
    oq = np.zeros((1, HEADS * HEAD_PAD), np.float32)
    ok = np.zeros_like(oq)
    for h in range(HEADS):
        for p in range(N_SPLIT):
            eq[p * LANES + h, h * HEAD_PAD + FOX_AUG + p] = 1.0
            ek[p * LANES + h, h * HEAD_PAD + FOX_AUG + N_SPLIT + p] = -1.0
            oq[0, h * HEAD_PAD + FOX_AUG + N_SPLIT + p] = 1.0
            ok[0, h * HEAD_PAD + FOX_AUG + p] = 1.0
    return jnp.asarray(eq, BF16), jnp.asarray(ek, BF16), jnp.asarray(oq), jnp.asarray(ok)


def kernel(x, meta, attn_norm, w_in, b_forget, q_a_norm, w_q_up, kv_a_norm, w_kv_up, w_mla_out, w_fox_out, w_out, ffn_norm, w_group_router, b_group_router, w_expert_router, b_expert_router, w_gate, w_up, w_down, final_norm):
    b, seq, d = x.shape
    t_attn = 512
    row = lambda v: v.reshape(1, -1).astype(F32)

    w = w_in[0]
    offs = np.cumsum([0, Q_LORA, KV_LORA, MLA_ROPE, HEADS * FOX_HD, HEADS * FOX_HD, HEADS * FOX_HD,
                      HEADS, D_MODEL, D_MODEL])
    seg = lambda j: w[:, offs[j]:offs[j + 1]]
    zcol = lambda n: jnp.zeros((d, n), F32)
    w_a = jnp.concatenate(
        [seg(0), seg(1),
         zcol(ROPE_LO), seg(2), zcol(LANES - ROPE_LO - MLA_ROPE),
         seg(6), zcol(LANES - HEADS),
         _pad_heads(seg(3), FOX_HD), _pad_heads(seg(4), FOX_HD)], axis=1).astype(BF16)
    w_fvt = seg(5).T.astype(BF16)
    w_gates = jnp.concatenate([seg(7), seg(8)], axis=1).astype(BF16)
    w_q = _pad_heads(w_q_up[0], MLA_NOPE + MLA_ROPE).astype(BF16)
    kv3 = w_kv_up[0].reshape(KV_LORA, HEADS, MLA_NOPE + MLA_V)
    w_k = _pad_heads(kv3[:, :, :MLA_NOPE].reshape(KV_LORA, HEADS * MLA_NOPE), MLA_NOPE).astype(BF16)
    w_vt = kv3[:, :, MLA_NOPE:].reshape(KV_LORA, HEADS * MLA_V).T.astype(BF16)
    bf128 = jnp.pad(row(b_forget[0]), ((0, 0), (0, LANES - HEADS)))
    eq, ek, oq, ok = _decay_columns()
    cos, sin = _rope_tables(N_META + seq)
    consts = [row(attn_norm[0]), w_a, row(q_a_norm[0]), w_q, row(kv_a_norm[0]), w_k, w_vt, w_fvt, bf128,
              eq, ek, oq, ok]

    meta_pad = jnp.pad(meta.astype(F32), ((0, META_PAD - N_META), (0, 0)))[None]
    pad_tab = lambda tab: jnp.pad(tab[:N_META], ((0, META_PAD - N_META), (0, 0)))
    _, km_meta, vm_meta, _, kf_meta, vf_meta, c_meta = _projections(
        meta_pad, pad_tab(cos), pad_tab(sin), jnp.zeros((1, LANES), F32), consts, tm=META_PAD)
    c0 = c_meta[0, N_META - 1:N_META, :]

    qm, km, vm, qf, kf, vf, _ = _projections(x, cos[N_META:], sin[N_META:], c0, consts, tm=t_attn)
    ya_t = _attention(qm, km, vm, km_meta, vm_meta, t=ATTN_QUERIES, tk=t_attn, g=ATTN_HEADS_PER_STEP)
    yb_t = _attention(qf, kf, vf, kf_meta, vf_meta, t=ATTN_QUERIES, tk=t_attn, g=ATTN_HEADS_PER_STEP)

    w_r = jnp.concatenate([w_expert_router[0], w_group_router[0],
                           jnp.zeros((d, LANES - N_EXPERTS - N_GROUPS), F32)], axis=1)
    w_rh = w_r.astype(BF16)
    w_rl = (w_r - w_rh.astype(F32)).astype(BF16)
    b_r = jnp.pad(jnp.concatenate([row(b_expert_router[0]), row(b_group_router[0])], axis=1),
                  ((0, 0), (0, LANES - N_EXPERTS - N_GROUPS)))
    h1, t_tiles, ri, rw, cnt = _merge(
        x, ya_t, yb_t,
        [row(attn_norm[0]), w_gates, w_mla_out[0].astype(BF16), w_fox_out[0].astype(BF16),
         w_out[0].astype(BF16), row(ffn_norm[0]), w_rh, w_rl, b_r], tm=512)

    n = b * seq
    counts = cnt[0, :N_EXPERTS].astype(jnp.int32)
    row_start, padded, tiles_used, tile_expert, tile_valid, nt = _tile_plan(counts, n, tm=EXPERT_ROWS)
    dst = _dst_rows(row_start, ri)
    xs = _dispatch(row_start, counts, padded, tiles_used, dst, t_tiles, nt, tm=512, te=EXPERT_ROWS)
    y_tiles = _experts(tile_expert, tile_valid, xs,
                       w_gate[0].reshape(N_EXPERTS, d, D_EXPERT), w_up[0].reshape(N_EXPERTS, d, D_EXPERT),
                       w_down[0].reshape(N_EXPERTS, D_EXPERT, d), tm=EXPERT_ROWS)
    out = _combine(dst, y_tiles.reshape(-1, ROW_TILE, LANES), h1.reshape(n, d), rw.reshape(n, LANES),
                   row(final_norm), tc=COMBINE_TOKENS)
    return out.reshape(b, seq, d)
```

```python
import functools

import numpy as np
import jax
import jax.numpy as jnp
from jax import lax
from jax.experimental import pallas as pl
from jax.experimental.pallas import tpu as pltpu

F32 = jnp.float32
BF16 = jnp.bfloat16

D_MODEL = 1024
N_META = 16
HEADS = 8
MLA_NOPE, MLA_ROPE, MLA_V = 64, 32, 64
Q_LORA, KV_LORA = 512, 256
ROPE_THETA = 10000.0
FOX_HD = 64
N_GROUPS, EXPERTS_PER_GROUP, D_EXPERT = 4, 8, 256
N_EXPERTS = N_GROUPS * EXPERTS_PER_GROUP
EPS = 1e-6
NEG = -1e30
LOG2E = 1.4426950408889634

LANES = 128
HEAD_PAD = LANES
META_PAD = LANES
ROPE_LO = MLA_NOPE
ROPE_HALF = MLA_ROPE // 2
FOX_AUG = FOX_HD
N_SPLIT = 3

VMEM_LIMIT = 56 * 1024 * 1024
ATTN_QUERIES = 1024
ATTN_HEADS_PER_STEP = 8
SKEW = 2
ACC_PAD = 16
ROW_TILE = D_MODEL // LANES
ROW_PITCH = ROW_TILE + 1
IDX_FIELDS = 4
EXPERT_ROWS = 256
COMBINE_TOKENS = 256

_NT = (((1,), (1,)), ((), ()))
_TN = (((0,), (0,)), ((), ()))


def _rms(x, g):
    return x * lax.rsqrt(jnp.mean(x * x, axis=-1, keepdims=True) + EPS) * g


def _rows_to_tiles(ref2d, v, pitch=ROW_TILE):
    n = v.shape[0]
    for c in range(ROW_TILE):
        ref2d[pl.ds(c, n, stride=pitch), :] = v[:, c * LANES:(c + 1) * LANES]
    for c in range(ROW_TILE, pitch):
        ref2d[pl.ds(c, n, stride=pitch), :] = jnp.zeros((n, LANES), v.dtype)


def _rows_from_tiles(ref2d, first, n, pitch):
    return jnp.concatenate([ref2d[pl.ds(first + c, n, stride=pitch), :] for c in range(ROW_TILE)], axis=1)


def _split3(v):
    hi = v.astype(BF16)
    r = v - hi.astype(F32)
    mid = r.astype(BF16)
    lo = (r - mid.astype(F32)).astype(BF16)
    return hi, mid, lo


def _rope_tile(v, c, s):
    lane = lax.broadcasted_iota(jnp.int32, v.shape, 1)
    partner = jnp.where(lane < ROPE_LO + ROPE_HALF,
                        pltpu.roll(v, LANES - ROPE_HALF, 1), pltpu.roll(v, ROPE_HALF, 1))
    return v * c + partner * s


def _proj_kernel(x_ref, cos_ref, sin_ref, c0_ref, an_ref, wa_ref, qn_ref, wq_ref, kvn_ref, wk_ref,
                 wvt_ref, wfvt_ref, bf_ref, eq_ref, ek_ref, oq_ref, ok_ref,
                 qm_ref, km_ref, vm_ref, qf_ref, kf_ref, vf_ref, cl_ref, carry_ref, *, tm):
    i = pl.program_id(1)

    @pl.when(i == 0)
    def _():
        carry_ref[...] = jnp.broadcast_to(c0_ref[...], carry_ref.shape)

    ub = _rms(x_ref[0], an_ref[...]).astype(BF16)
    pa = jnp.dot(ub, wa_ref[...], preferred_element_type=F32)
    c_q = pa[:, 0:Q_LORA]
    c_kv = pa[:, Q_LORA:Q_LORA + KV_LORA]
    o = Q_LORA + KV_LORA
    kr = pa[:, o:o + LANES]
    fl = pa[:, o + LANES:o + 2 * LANES]
    o += 2 * LANES
    fq = pa[:, o:o + HEADS * FOX_HD]
    fk = pa[:, o + HEADS * FOX_HD:o + 2 * HEADS * FOX_HD]

    cos = cos_ref[...]
    sin = sin_ref[...]

    cqn = _rms(c_q, qn_ref[...]).astype(BF16)
    q = jnp.dot(cqn, wq_ref[...], preferred_element_type=F32)
    ckvn = _rms(c_kv, kvn_ref[...]).astype(BF16)
    kb = jnp.dot(ckvn, wk_ref[...], preferred_element_type=F32)
    krr = _rope_tile(kr, cos, sin)
    q_scale = (MLA_NOPE + MLA_ROPE) ** -0.5 * LOG2E
    lane = lax.broadcasted_iota(jnp.int32, krr.shape, 1)
    for h in range(HEADS):
        sl = slice(h * HEAD_PAD, (h + 1) * HEAD_PAD)
        qm_ref[0, h] = (_rope_tile(q[:, sl], cos, sin) * q_scale).astype(BF16)
        kh = kb[:, (h // 2) * LANES:(h // 2 + 1) * LANES]
        if h % 2:
            kh = pltpu.roll(kh, LANES - MLA_NOPE, 1)
        km_ref[0, h] = jnp.where(lane < MLA_NOPE, kh, krr).astype(BF16)
    vt = lax.dot_general(wvt_ref[...], ckvn, _NT, preferred_element_type=F32)
    for h in range(HEADS):
        vm_ref[0, h, 0] = vt[h * MLA_V:(h + 1) * MLA_V, :].astype(BF16)

    z = fl + bf_ref[...]
    logf = jnp.minimum(z, 0.0) - jnp.log1p(jnp.exp(-jnp.abs(z)))
    r_i = lax.broadcasted_iota(jnp.int32, (tm, tm), 0)
    c_i = lax.broadcasted_iota(jnp.int32, (tm, tm), 1)
    tri = jnp.where(c_i <= r_i, 1.0, 0.0).astype(BF16)
    hi, mid, lo = _split3(logf)
    cs2 = jnp.dot(tri, jnp.concatenate([hi, mid], axis=1), preferred_element_type=F32)
    cs = cs2[:, :LANES] + cs2[:, LANES:] + jnp.dot(tri, lo, preferred_element_type=F32)
    c = carry_ref[0:1, :] + cs
    carry_ref[...] = jnp.broadcast_to(c[tm - 1:tm, :], carry_ref.shape)
    cl_ref[0] = c
    p_hi, p_mid, p_lo = (piece.astype(F32) for piece in _split3(c * LOG2E))
    cat = jnp.where(lane < HEADS, p_hi,
                    jnp.where(lane < 2 * HEADS, pltpu.roll(p_mid, HEADS, 1),
                              jnp.where(lane < 3 * HEADS, pltpu.roll(p_lo, 2 * HEADS, 1), 0.0))).astype(BF16)
    augq = jnp.dot(cat, eq_ref[...], preferred_element_type=F32) + oq_ref[...]
    augk = jnp.dot(cat, ek_ref[...], preferred_element_type=F32) + ok_ref[...]
    fqs = fq * (FOX_HD ** -0.5 * LOG2E)
    for h in range(HEADS):
        pair = slice((h // 2) * LANES, (h // 2 + 1) * LANES)
        sl = slice(h * HEAD_PAD, (h + 1) * HEAD_PAD)
        qh, kh = fqs[:, pair], fk[:, pair]
        if h % 2:
            qh, kh = pltpu.roll(qh, LANES - FOX_HD, 1), pltpu.roll(kh, LANES - FOX_HD, 1)
        qf_ref[0, h] = jnp.where(lane < FOX_HD, qh, augq[:, sl]).astype(BF16)
        kf_ref[0, h] = jnp.where(lane < FOX_HD, kh, augk[:, sl]).astype(BF16)
    vft = lax.dot_general(wfvt_ref[...], ub, _NT, preferred_element_type=F32)
    for h in range(HEADS):
        vf_ref[0, h, 0] = vft[h * FOX_HD:(h + 1) * FOX_HD, :].astype(BF16)


def _projections(x3, cos, sin, c0, consts, *, tm):
    b, l, _ = x3.shape
    nt = l // tm
    full = lambda a: pl.BlockSpec(a.shape, lambda bi, i: (0,) * a.ndim)
    qk_shape = jax.ShapeDtypeStruct((b, HEADS, l, HEAD_PAD), BF16)
    vt_shape = jax.ShapeDtypeStruct((b, HEADS, nt, MLA_V, tm), BF16)
    qk_spec = pl.BlockSpec((1, HEADS, tm, HEAD_PAD), lambda bi, i: (bi, 0, i, 0))
    vt_spec = pl.BlockSpec((1, HEADS, 1, MLA_V, tm), lambda bi, i: (bi, 0, i, 0, 0))
    return pl.pallas_call(
        functools.partial(_proj_kernel, tm=tm),
        grid=(b, nt),
        in_specs=[pl.BlockSpec((1, tm, D_MODEL), lambda bi, i: (bi, i, 0)),
                  pl.BlockSpec((tm, LANES), lambda bi, i: (i, 0)),
                  pl.BlockSpec((tm, LANES), lambda bi, i: (i, 0)),
                  full(c0)] + [full(a) for a in consts],
        out_specs=[qk_spec, qk_spec, vt_spec, qk_spec, qk_spec, vt_spec,
                   pl.BlockSpec((1, tm, LANES), lambda bi, i: (bi, i, 0))],
        out_shape=[qk_shape, qk_shape, vt_shape, qk_shape, qk_shape, vt_shape,
                   jax.ShapeDtypeStruct((b, l, LANES), F32)],
        scratch_shapes=[pltpu.VMEM((8, LANES), F32)],
        compiler_params=pltpu.CompilerParams(
            dimension_semantics=("arbitrary", "arbitrary"), vmem_limit_bytes=VMEM_LIMIT),
        name="proj",
    )(x3, cos, sin, c0, *consts)


def _attn_kernel(q_ref, k_ref, vt_ref, km_ref, vtm_ref, o_ref, m_ref, acc_ref, *, t, tk, g):
    i = pl.program_id(2)
    dv = vt_ref.shape[3]

    def ones_rows(n):
        return jnp.where(lax.broadcasted_iota(jnp.int32, (ACC_PAD, n), 0) == 0, 1.0, 0.0).astype(BF16)

    def update(hd, s, vt_blk, first=False, lo=0):
        m_cur = jnp.max(s, axis=0, keepdims=True)
        if first:
            m_new = m_cur
        else:
            m_prev = m_ref[hd, :, lo:]
            m_new = jnp.maximum(m_prev, m_cur)
            alpha = jnp.exp2(m_prev - m_new)
        p = jnp.exp2(s - m_new).astype(BF16)
        vt_aug = jnp.concatenate([vt_blk, ones_rows(p.shape[0])], axis=0)
        pv = jnp.dot(vt_aug, p, preferred_element_type=F32)
        acc_ref[hd, :, lo:] = pv if first else alpha * acc_ref[hd, :, lo:] + pv
        m_ref[hd, :, lo:] = m_new

    def scores(hd, k, lo=0):
        return lax.dot_general(k, q_ref[0, hd, lo:, :], _NT, preferred_element_type=F32)

    def keys(j):
        off = pl.multiple_of(j * tk, tk)
        return lambda hd: k_ref[0, hd, pl.ds(off, tk), :]

    def sweep(keys_of, vt_of, mask, first=False, lo=0):
        pending = [scores(hd, keys_of(hd), lo) for hd in range(SKEW)]
        for hd in range(g):
            if hd + SKEW < g:
                pending.append(scores(hd + SKEW, keys_of(hd + SKEW), lo))
            s = pending.pop(0)
            if mask is not None:
                s = jnp.where(mask, s, NEG)
            update(hd, s, vt_of(hd), first=first, lo=lo)

    row = lax.broadcasted_iota(jnp.int32, (META_PAD, t), 0)
    sweep(lambda hd: km_ref[0, hd], lambda hd: vtm_ref[0, hd, 0], row < N_META, first=True)

    def body(j, carry):
        sweep(keys(j), lambda hd: vt_ref[0, hd, j], None)
        return carry

    per = t // tk
    lax.fori_loop(0, i * per, body, 0)

    for d in range(per):
        jd = i * per + d
        key_row = lax.broadcasted_iota(jnp.int32, (tk, t - d * tk), 0)
        query = lax.broadcasted_iota(jnp.int32, (tk, t - d * tk), 1)
        sweep(keys(jd), lambda hd, jd=jd: vt_ref[0, hd, jd], key_row <= query, lo=d * tk)

    for hd in range(g):
        a = acc_ref[hd]
        o_ref[0, hd * dv:(hd + 1) * dv, :] = (a[:dv] / a[dv:dv + 1]).astype(o_ref.dtype)


def _attention(q, k, vt, k_meta, vt_meta, *, t, tk, g):
    b, h, l, _ = q.shape
    nq = l // t
    dv = vt.shape[3]
    return pl.pallas_call(
        functools.partial(_attn_kernel, t=t, tk=tk, g=g),
        grid=(b, h // g, nq),
        in_specs=[pl.BlockSpec((1, g, t, HEAD_PAD), lambda bi, hi, i: (bi, hi, i, 0)),
                  pl.BlockSpec((1, g, l, HEAD_PAD), lambda bi, hi, i: (bi, hi, 0, 0),
                               pipeline_mode=pl.Buffered(1)),
                  pl.BlockSpec((1, g, l // tk, dv, tk), lambda bi, hi, i: (bi, hi, 0, 0, 0),
                               pipeline_mode=pl.Buffered(1)),
                  pl.BlockSpec((1, g, META_PAD, HEAD_PAD), lambda bi, hi, i: (0, hi, 0, 0)),
                  pl.BlockSpec((1, g, 1, dv, META_PAD), lambda bi, hi, i: (0, hi, 0, 0, 0))],
        out_specs=pl.BlockSpec((1, g * dv, t), lambda bi, hi, i: (bi, hi, i)),
        out_shape=jax.ShapeDtypeStruct((b, h * dv, l), BF16),
        scratch_shapes=[pltpu.VMEM((g, 1, t), F32), pltpu.VMEM((g, dv + ACC_PAD, t), F32)],
        compiler_params=pltpu.CompilerParams(
            dimension_semantics=("arbitrary", "arbitrary", "arbitrary"), vmem_limit_bytes=VMEM_LIMIT),
        name="attn",
    )(q, k, vt, k_meta, vt_meta)


def _merge_kernel(x_ref, ya_ref, yb_ref, an_ref, wg_ref, wmo_ref, wfo_ref, wo_ref, fn_ref,
                  wrhl_ref, br_ref, h1_ref, t_ref, ri_ref, rw_ref, cnt_ref):
    @pl.when((pl.program_id(0) == 0) & (pl.program_id(1) == 0))
    def _():
        cnt_ref[...] = jnp.zeros_like(cnt_ref)

    x = x_ref[0]
    ub = _rms(x, an_ref[...]).astype(BF16)
    g = jnp.dot(ub, wg_ref[...], preferred_element_type=F32)
    ya = lax.dot_general(ya_ref[0], wmo_ref[...], _TN, preferred_element_type=F32)
    yb = lax.dot_general(yb_ref[0], wfo_ref[...], _TN, preferred_element_type=F32)
    merged = jax.nn.sigmoid(g[:, :D_MODEL]) * ya + jax.nn.sigmoid(g[:, D_MODEL:]) * yb
    h1 = x + jnp.dot(merged.astype(BF16), wo_ref[...], preferred_element_type=F32)
    h1_ref[0] = h1
    t = _rms(h1, fn_ref[...])
    _rows_to_tiles(t_ref, t, ROW_PITCH)

    th = t.astype(BF16)
    tl = (t - th.astype(F32)).astype(BF16)
    hl = jnp.dot(th, wrhl_ref[...], preferred_element_type=F32)
    logits = (hl[:, :LANES] + hl[:, LANES:]
              + jnp.dot(tl, wrhl_ref[:, :LANES], preferred_element_type=F32)) + br_ref[...]
    lane = lax.broadcasted_iota(jnp.int32, logits.shape, 1)
    big = jnp.int32(LANES)

    def first_argmax(v, vmax):
        return jnp.min(jnp.where(v == vmax, lane, big), axis=-1, keepdims=True)

    gmask = (lane >= N_EXPERTS) & (lane < N_EXPERTS + N_GROUPS)
    gl = jnp.where(gmask, logits, NEG)
    gmax = jnp.max(gl, axis=-1, keepdims=True)
    g_w = 1.0 / jnp.sum(jnp.where(gmask, jnp.exp(gl - gmax), 0.0), axis=-1, keepdims=True)
    g_idx = first_argmax(gl, gmax) - N_EXPERTS
    emask = (lane < N_EXPERTS) & (jnp.right_shift(lane, 3) == g_idx)
    el = jnp.where(emask, logits, NEG)
    m1 = jnp.max(el, axis=-1, keepdims=True)
    i1 = first_argmax(el, m1)
    el2 = jnp.where(lane == i1, NEG, el)
    m2 = jnp.max(el2, axis=-1, keepdims=True)
    i2 = first_argmax(el2, m2)
    zsum = jnp.sum(jnp.where(emask, jnp.exp(el - m1), 0.0), axis=-1, keepdims=True)
    p1 = 1.0 / zsum
    p2 = jnp.exp(m2 - m1) / zsum
    tot = p1 + p2
    w1 = g_w * (p1 / tot)
    w2 = g_w * (p2 / tot)
    rw_ref[0] = jnp.where(lane == 0, w1, jnp.where(lane == 1, w2, 0.0))

    tm = logits.shape[0]
    hot1 = lane == i1
    hot2 = lane == i2
    onehot = jnp.where(hot1, 1.0, jnp.where(hot2, 1.0, 0.0))
    r_i = lax.broadcasted_iota(jnp.int32, (tm, tm), 0)
    c_i = lax.broadcasted_iota(jnp.int32, (tm, tm), 1)
    before = jnp.where(c_i < r_i, 1.0, 0.0).astype(BF16)
    rank = cnt_ref[0:1, :] + jnp.dot(before, onehot.astype(BF16), preferred_element_type=F32)
    rank1 = jnp.sum(jnp.where(hot1, rank, 0.0), axis=-1, keepdims=True).astype(jnp.int32)
    rank2 = jnp.sum(jnp.where(hot2, rank, 0.0), axis=-1, keepdims=True).astype(jnp.int32)
    fields = jnp.where(lane == 0, i1, jnp.where(lane == 1, i2,
                       jnp.where(lane == 2, rank1, jnp.where(lane == 3, rank2, 0))))
    ri_ref[0] = fields.T[0:IDX_FIELDS, :]
    cnt_ref[...] = cnt_ref[...] + jnp.sum(onehot, axis=0, keepdims=True)


def _merge(x3, ya_t, yb_t, consts, *, tm):
    b, l, _ = x3.shape
    full = lambda a: pl.BlockSpec(a.shape, lambda bi, i: (0,) * a.ndim)
    tok = lambda w: pl.BlockSpec((1, tm, w), lambda bi, i: (bi, i, 0))
    yspec = pl.BlockSpec((1, ya_t.shape[1], tm), lambda bi, i: (bi, 0, i))
    return pl.pallas_call(
        _merge_kernel,
        grid=(b, l // tm),
        in_specs=[tok(D_MODEL), yspec, yspec] + [full(a) for a in consts],
        out_specs=[tok(D_MODEL),
                   pl.BlockSpec((tm * ROW_PITCH, LANES), lambda bi, i: (bi * (l // tm) + i, 0)),
                   pl.BlockSpec((1, IDX_FIELDS, tm), lambda bi, i: (bi * (l // tm) + i, 0, 0)), tok(LANES),
                   pl.BlockSpec((8, LANES), lambda bi, i: (0, 0))],
        out_shape=[jax.ShapeDtypeStruct((b, l, D_MODEL), F32),
                   jax.ShapeDtypeStruct((b * l * ROW_PITCH, LANES), F32),
                   jax.ShapeDtypeStruct((b * (l // tm), IDX_FIELDS, tm), jnp.int32),
                   jax.ShapeDtypeStruct((b, l, LANES), F32),
                   jax.ShapeDtypeStruct((8, LANES), F32)],
        compiler_params=pltpu.CompilerParams(
            dimension_semantics=("arbitrary", "arbitrary"), vmem_limit_bytes=VMEM_LIMIT),
        name="merge",
    )(x3, ya_t, yb_t, *consts)


def _dst_rows_kernel(rs_ref, idx_ref, d_ref):
    for k in range(2):
        e = idx_ref[:, k, :]
        start = jnp.zeros_like(e)
        for j in range(N_EXPERTS):
            start = jnp.where(e == j, rs_ref[j], start)
        d_ref[:, k, :] = start + idx_ref[:, 2 + k, :]


def _dst_rows(row_start, idx):
    nt, _, tm = idx.shape
    return pl.pallas_call(
        _dst_rows_kernel,
        grid_spec=pltpu.PrefetchScalarGridSpec(
            num_scalar_prefetch=1, grid=(1,),
            in_specs=[pl.BlockSpec(idx.shape, lambda i, rs: (0, 0, 0))],
            out_specs=pl.BlockSpec((nt, 2, tm), lambda i, rs: (0, 0, 0))),
        out_shape=jax.ShapeDtypeStruct((nt, 2, tm), jnp.int32),
        name="dst_rows",
    )(row_start, idx)


def _dispatch_kernel(rs_ref, cnt_ref, pad_ref, used_ref, d_ref, t_ref, xs_hbm, zero_ref, sem, zsem, *,
                     tm, te, nt):
    def row_tile(row):
        return xs_hbm.at[pl.ds(row * ROW_PITCH, ROW_PITCH)]

    def zero_row(row):
        return pltpu.make_async_copy(zero_ref.at[pl.ds(0, ROW_PITCH)], row_tile(row), zsem)

    def zero_tile(k):
        return pltpu.make_async_copy(zero_ref, xs_hbm.at[pl.ds(k * (te * ROW_PITCH), te * ROW_PITCH)], zsem)

    @pl.when(pl.program_id(0) == 0)
    def _():
        zero_ref[...] = jnp.zeros_like(zero_ref)
        for e in range(N_EXPERTS):
            def fill(p, carry, e=e):
                zero_row(rs_ref[e] + p).start()
                return carry
            lax.fori_loop(cnt_ref[e], pad_ref[e], fill, 0)

        def fill_tile(k, carry):
            zero_tile(k).start()
            return carry
        lax.fori_loop(used_ref[0], nt, fill_tile, 0)
        for e in range(N_EXPERTS):
            def drain(p, carry):
                zero_row(0).wait()
                return carry
            lax.fori_loop(cnt_ref[e], pad_ref[e], drain, 0)

        def drain_tile(k, carry):
            zero_tile(0).wait()
            return carry
        lax.fori_loop(used_ref[0], nt, drain_tile, 0)

    def copy(r, k):
        return pltpu.make_async_copy(t_ref.at[pl.ds(r * ROW_PITCH, ROW_PITCH)], row_tile(d_ref[0, k, r]), sem)

    def row(r, carry):
        copy(r, 0).start()
        copy(r, 1).start(priority=1)
        return carry
    lax.fori_loop(0, tm, row, 0, unroll=8)
    for _ in range(2 * tm):
        copy(0, 0).wait()


def _dispatch(row_start, counts, padded, tiles_used, dst, t_tiles2d, n_tiles, *, tm, te):
    grid_spec = pltpu.PrefetchScalarGridSpec(
        num_scalar_prefetch=4,
        grid=(dst.shape[0],),
        in_specs=[pl.BlockSpec((1, 2, tm), lambda i, *_: (i, 0, 0), memory_space=pltpu.SMEM),
                  pl.BlockSpec((tm * ROW_PITCH, LANES), lambda i, *_: (i, 0))],
        out_specs=pl.BlockSpec(memory_space=pl.ANY),
        scratch_shapes=[pltpu.VMEM((te * ROW_PITCH, LANES), F32), pltpu.SemaphoreType.DMA(()),
                        pltpu.SemaphoreType.DMA(())])
    return pl.pallas_call(
        functools.partial(_dispatch_kernel, tm=tm, te=te, nt=n_tiles),
        grid_spec=grid_spec,
        out_shape=jax.ShapeDtypeStruct((n_tiles * te * ROW_PITCH, LANES), F32),
        compiler_params=pltpu.CompilerParams(dimension_semantics=("arbitrary",), vmem_limit_bytes=VMEM_LIMIT),
        name="dispatch",
    )(row_start, counts, padded, tiles_used, dst, t_tiles2d)


def _expert_kernel(te_ref, tv_ref, x_ref, wg_ref, wu_ref, wd_ref, y_ref, *, tm):
    del te_ref
    i = pl.program_id(0)

    @pl.when(tv_ref[i] > 0)
    def _():
        x = _rows_from_tiles(x_ref, 0, tm, ROW_PITCH).astype(BF16)
        a = jnp.dot(x, wg_ref[0].astype(BF16), preferred_element_type=F32)
        u = jnp.dot(x, wu_ref[0].astype(BF16), preferred_element_type=F32)
        hm = (a * jax.nn.sigmoid(a)) * u
        _rows_to_tiles(y_ref, jnp.dot(hm.astype(BF16), wd_ref[0].astype(BF16), preferred_element_type=F32))

    @pl.when(tv_ref[i] == 0)
    def _():
        y_ref[...] = jnp.zeros_like(y_ref)


def _experts(tile_expert, tile_valid, xs, w_gate, w_up, w_down, *, tm):
    nt = tile_expert.shape[0]
    used = lambda i, tv: jnp.where(tv[i] > 0, i, 0)
    wspec = lambda shape: pl.BlockSpec((1,) + shape, lambda i, te, tv: (te[i], 0, 0))
    grid_spec = pltpu.PrefetchScalarGridSpec(
        num_scalar_prefetch=2,
        grid=(nt,),
        in_specs=[pl.BlockSpec((tm * ROW_PITCH, LANES), lambda i, te, tv: (used(i, tv), 0)),
                  wspec((D_MODEL, D_EXPERT)), wspec((D_MODEL, D_EXPERT)), wspec((D_EXPERT, D_MODEL))],
        out_specs=pl.BlockSpec((tm * ROW_TILE, LANES), lambda i, te, tv: (i, 0)))
    return pl.pallas_call(
        functools.partial(_expert_kernel, tm=tm),
        grid_spec=grid_spec,
        out_shape=jax.ShapeDtypeStruct((nt * tm * ROW_TILE, LANES), F32),
        compiler_params=pltpu.CompilerParams(dimension_semantics=("arbitrary",), vmem_limit_bytes=VMEM_LIMIT),
        name="experts",
    )(tile_expert, tile_valid, xs, w_gate, w_up, w_down)


def _combine_kernel(d_ref, d_next_ref, y_hbm, h1_ref, rw_ref, fn_ref, o_ref, ybuf, sem, *, tc, nt):
    i = pl.program_id(0)
    slot = lax.rem(i, 2)
    half = 2 * tc * ROW_PITCH

    def copy(dr, first, s, r, k):
        return pltpu.make_async_copy(y_hbm.at[dr[0, k, r]],
                                     ybuf.at[pl.ds(first + (k * tc + r) * ROW_PITCH, ROW_TILE)], sem.at[s])

    def gather(dr, first, s):
        def row(r, carry):
            copy(dr, first, s, r, 0).start()
            copy(dr, first, s, r, 1).start(priority=1)
            return carry
        lax.fori_loop(0, tc, row, 0, unroll=8)

    @pl.when(i == 0)
    def _():
        gather(d_ref, 0, 0)

    @pl.when(i + 1 < nt)
    def _():
        gather(d_next_ref, (1 - slot) * half, 1 - slot)

    for _ in range(2 * tc):
        copy(d_ref, slot * half, slot, 0, 0).wait()
    rw = rw_ref[...]
    y = (rw[:, 0:1] * _rows_from_tiles(ybuf, slot * half, tc, ROW_PITCH)
         + rw[:, 1:2] * _rows_from_tiles(ybuf, slot * half + tc * ROW_PITCH, tc, ROW_PITCH))
    o_ref[...] = _rms(h1_ref[...] + y, fn_ref[...])


def _combine(dst, y_tiles, h2, rw2, final_norm, *, tc):
    n = h2.shape[0]
    nt = n // tc
    per = dst.shape[2] // tc
    blk = lambda i: (i // per, 0, lax.rem(i, per))
    return pl.pallas_call(
        functools.partial(_combine_kernel, tc=tc, nt=nt),
        grid=(nt,),
        in_specs=[pl.BlockSpec((1, 2, tc), lambda i: blk(i), memory_space=pltpu.SMEM),
                  pl.BlockSpec((1, 2, tc), lambda i: blk(jnp.minimum(i + 1, nt - 1)), memory_space=pltpu.SMEM),
                  pl.BlockSpec(memory_space=pl.ANY),
                  pl.BlockSpec((tc, D_MODEL), lambda i: (i, 0)),
                  pl.BlockSpec((tc, LANES), lambda i: (i, 0)),
                  pl.BlockSpec((1, D_MODEL), lambda i: (0, 0))],
        out_specs=pl.BlockSpec((tc, D_MODEL), lambda i: (i, 0)),
        out_shape=jax.ShapeDtypeStruct((n, D_MODEL), F32),
        scratch_shapes=[pltpu.VMEM((2 * 2 * tc * ROW_PITCH, LANES), F32), pltpu.SemaphoreType.DMA((2,))],
        compiler_params=pltpu.CompilerParams(dimension_semantics=("arbitrary",), vmem_limit_bytes=VMEM_LIMIT),
        name="combine",
    )(dst, dst, y_tiles, h2, rw2, final_norm)


def _tile_plan(counts, n, *, tm):
    nt = (2 * n) // tm + N_EXPERTS
    tiles_e = (counts + tm - 1) // tm
    tile_end = jnp.cumsum(tiles_e)
    tile_start = tile_end - tiles_e
    k = jnp.arange(nt, dtype=jnp.int32)
    tile_expert = jnp.minimum(jnp.sum((k[:, None] >= tile_end[None, :]).astype(jnp.int32), axis=1), N_EXPERTS - 1)
    onehot = (tile_expert[:, None] == jnp.arange(N_EXPERTS)[None, :]).astype(jnp.int32)
    left = jnp.sum(onehot * (counts[None, :] - (k[:, None] - tile_start[None, :]) * tm), axis=1)
    tile_valid = jnp.where(k < tile_end[-1], jnp.clip(left, 0, tm), 0).astype(jnp.int32)
    return ((tile_start * tm).astype(jnp.int32), (tiles_e * tm).astype(jnp.int32),
            tile_end[-1:].astype(jnp.int32), tile_expert.astype(jnp.int32), tile_valid, nt)


def _pad_heads(w, width):
    k = w.shape[0]
    return jnp.pad(w.reshape(k, HEADS, width), ((0, 0), (0, 0), (0, HEAD_PAD - width))).reshape(k, HEADS * HEAD_PAD)


def _rope_tables(length):
    pos = np.arange(length, dtype=np.float64)
    inv = ROPE_THETA ** (-np.arange(0, MLA_ROPE, 2, dtype=np.float64) / MLA_ROPE)
    ang = pos[:, None] * inv[None, :]
    cos, sin = np.cos(ang), np.sin(ang)
    one = np.ones((length, ROPE_LO))
    zero_lo = np.zeros((length, ROPE_LO))
    zero_hi = np.zeros((length, LANES - ROPE_LO - MLA_ROPE))
    return (np.concatenate([one, cos, cos, zero_hi], axis=1).astype(np.float32),
            np.concatenate([zero_lo, -sin, sin, zero_hi], axis=1).astype(np.float32))


def _decay_columns():
    eq = np.zeros((LANES, HEADS * HEAD_PAD), np.float32)
    ek = np.zeros_like(eq)
    oq = np.zeros((1, HEADS * HEAD_PAD), np.float32)
    ok = np.zeros_like(oq)
    for h in range(HEADS):
        for p in range(N_SPLIT):
            eq[p * HEADS + h, h * HEAD_PAD + FOX_AUG + p] = 1.0
            ek[p * HEADS + h, h * HEAD_PAD + FOX_AUG + N_SPLIT + p] = -1.0
            oq[0, h * HEAD_PAD + FOX_AUG + N_SPLIT + p] = 1.0
            ok[0, h * HEAD_PAD + FOX_AUG + p] = 1.0
    return jnp.asarray(eq, BF16), jnp.asarray(ek, BF16), jnp.asarray(oq), jnp.asarray(ok)


def kernel(x, meta, attn_norm, w_in, b_forget, q_a_norm, w_q_up, kv_a_norm, w_kv_up, w_mla_out, w_fox_out, w_out, ffn_norm, w_group_router, b_group_router, w_expert_router, b_expert_router, w_gate, w_up, w_down, final_norm):
    b, seq, d = x.shape
    t_attn = 512
    row = lambda v: v.reshape(1, -1).astype(F32)

    w = w_in[0]
    offs = np.cumsum([0, Q_LORA, KV_LORA, MLA_ROPE, HEADS * FOX_HD, HEADS * FOX_HD, HEADS * FOX_HD,
                      HEADS, D_MODEL, D_MODEL])
    seg = lambda j: w[:, offs[j]:offs[j + 1]]
    zcol = lambda n: jnp.zeros((d, n), F32)
    w_a = jnp.concatenate(
        [seg(0), seg(1),
         zcol(ROPE_LO), seg(2), zcol(LANES - ROPE_LO - MLA_ROPE),
         seg(6), zcol(LANES - HEADS),
         seg(3), seg(4)], axis=1).astype(BF16)
    w_fvt = seg(5).T.astype(BF16)
    w_gates = jnp.concatenate([seg(7), seg(8)], axis=1).astype(BF16)
    w_q = _pad_heads(w_q_up[0], MLA_NOPE + MLA_ROPE).astype(BF16)
    kv3 = w_kv_up[0].reshape(KV_LORA, HEADS, MLA_NOPE + MLA_V)
    w_k = kv3[:, :, :MLA_NOPE].reshape(KV_LORA, HEADS * MLA_NOPE).astype(BF16)
    w_vt = kv3[:, :, MLA_NOPE:].reshape(KV_LORA, HEADS * MLA_V).T.astype(BF16)
    bf128 = jnp.pad(row(b_forget[0]), ((0, 0), (0, LANES - HEADS)))
    eq, ek, oq, ok = _decay_columns()
    cos, sin = _rope_tables(N_META + seq)
    consts = [row(attn_norm[0]), w_a, row(q_a_norm[0]), w_q, row(kv_a_norm[0]), w_k, w_vt, w_fvt, bf128,
              eq, ek, oq, ok]

    meta_pad = jnp.pad(meta.astype(F32), ((0, META_PAD - N_META), (0, 0)))[None]
    pad_tab = lambda tab: np.pad(tab[:N_META], ((0, META_PAD - N_META), (0, 0)))
    _, km_meta, vm_meta, _, kf_meta, vf_meta, c_meta = _projections(
        meta_pad, pad_tab(cos), pad_tab(sin), jnp.zeros((1, LANES), F32), consts, tm=META_PAD)
    c0 = c_meta[0, N_META - 1:N_META, :]

    qm, km, vm, qf, kf, vf, _ = _projections(x, cos[N_META:], sin[N_META:], c0, consts, tm=t_attn)
    ya_t = _attention(qm, km, vm, km_meta, vm_meta, t=ATTN_QUERIES, tk=t_attn, g=ATTN_HEADS_PER_STEP)
    yb_t = _attention(qf, kf, vf, kf_meta, vf_meta, t=ATTN_QUERIES, tk=t_attn, g=ATTN_HEADS_PER_STEP)

    w_r = jnp.concatenate([w_expert_router[0], w_group_router[0],
                           jnp.zeros((d, LANES - N_EXPERTS - N_GROUPS), F32)], axis=1)
    w_rh = w_r.astype(BF16)
    w_rhl = jnp.concatenate([w_rh, (w_r - w_rh.astype(F32)).astype(BF16)], axis=1)
    b_r = jnp.pad(jnp.concatenate([row(b_expert_router[0]), row(b_group_router[0])], axis=1),
                  ((0, 0), (0, LANES - N_EXPERTS - N_GROUPS)))
    h1, t_tiles, ri, rw, cnt = _merge(
        x, ya_t, yb_t,
        [row(attn_norm[0]), w_gates, w_mla_out[0].astype(BF16), w_fox_out[0].astype(BF16),
         w_out[0].astype(BF16), row(ffn_norm[0]), w_rhl, b_r], tm=512)

    n = b * seq
    counts = cnt[0, :N_EXPERTS].astype(jnp.int32)
    row_start, padded, tiles_used, tile_expert, tile_valid, nt = _tile_plan(counts, n, tm=EXPERT_ROWS)
    dst = _dst_rows(row_start, ri)
    xs = _dispatch(row_start, counts, padded, tiles_used, dst, t_tiles, nt, tm=512, te=EXPERT_ROWS)
    y_tiles = _experts(tile_expert, tile_valid, xs,
                       w_gate[0].reshape(N_EXPERTS, d, D_EXPERT), w_up[0].reshape(N_EXPERTS, d, D_EXPERT),
                       w_down[0].reshape(N_EXPERTS, D_EXPERT, d), tm=EXPERT_ROWS)
    out = _combine(dst, y_tiles.reshape(-1, ROW_TILE, LANES), h1.reshape(n, d), rw.reshape(n, LANES),
                   row(final_norm), tc=COMBINE_TOKENS)
    return out.reshape(b, seq, d)
```

```python
import functools

import numpy as np
import jax
import jax.numpy as jnp
from jax import lax
from jax.experimental import pallas as pl
from jax.experimental.pallas import tpu as pltpu

F32 = jnp.float32
BF16 = jnp.bfloat16

D_MODEL = 1024
N_META = 16
HEADS = 8
MLA_NOPE, MLA_ROPE, MLA_V = 64, 32, 64
Q_LORA, KV_LORA = 512, 256
ROPE_THETA = 10000.0
FOX_HD = 64
N_GROUPS, EXPERTS_PER_GROUP, D_EXPERT = 4, 8, 256
N_EXPERTS = N_GROUPS * EXPERTS_PER_GROUP
EPS = 1e-6
NEG = -1e30
LOG2E = 1.4426950408889634

LANES = 128
HEAD_PAD = LANES
META_PAD = LANES
ROPE_LO = MLA_NOPE
ROPE_HALF = MLA_ROPE // 2
FOX_AUG = FOX_HD
N_SPLIT = 3

VMEM_LIMIT = 56 * 1024 * 1024
ATTN_QUERIES = 1024
ATTN_HEADS_PER_STEP = 8
SKEW = 2
ACC_PAD = 16
ROW_TILE = D_MODEL // LANES
ROW_PITCH = ROW_TILE + 1
IDX_FIELDS = 4
X_AHEAD = 2
EXPERT_ROWS = 256
COMBINE_TOKENS = 256

_NT = (((1,), (1,)), ((), ()))
_TN = (((0,), (0,)), ((), ()))


def _rms(x, g):
    return x * lax.rsqrt(jnp.mean(x * x, axis=-1, keepdims=True) + EPS) * g


def _rows_to_tiles(ref2d, v, pitch=ROW_TILE):
    n = v.shape[0]
    for c in range(ROW_TILE):
        ref2d[pl.ds(c, n, stride=pitch), :] = v[:, c * LANES:(c + 1) * LANES]
    for c in range(ROW_TILE, pitch):
        ref2d[pl.ds(c, n, stride=pitch), :] = jnp.zeros((n, LANES), v.dtype)


def _rows_from_tiles(ref2d, first, n, pitch):
    return jnp.concatenate([ref2d[pl.ds(first + c, n, stride=pitch), :] for c in range(ROW_TILE)], axis=1)


def _split3(v):
    hi = v.astype(BF16)
    r = v - hi.astype(F32)
    mid = r.astype(BF16)
    lo = (r - mid.astype(F32)).astype(BF16)
    return hi, mid, lo


def _rope_tile(v, c, s):
    lane = lax.broadcasted_iota(jnp.int32, v.shape, 1)
    partner = jnp.where(lane < ROPE_LO + ROPE_HALF,
                        pltpu.roll(v, LANES - ROPE_HALF, 1), pltpu.roll(v, ROPE_HALF, 1))
    return v * c + partner * s


def _proj_kernel(x_ref, cos_ref, sin_ref, c0_ref, an_ref, wa_ref, qn_ref, wq_ref, kvn_ref, wk_ref,
                 wvt_ref, wfvt_ref, bf_ref, eq_ref, ek_ref, oq_ref, ok_ref,
                 qm_ref, km_ref, vm_ref, qf_ref, kf_ref, vf_ref, cl_ref, carry_ref, *, tm):
    i = pl.program_id(1)

    @pl.when(i == 0)
    def _():
        carry_ref[...] = jnp.broadcast_to(c0_ref[...], carry_ref.shape)

    ub = _rms(x_ref[0], an_ref[...]).astype(BF16)
    pa = jnp.dot(ub, wa_ref[...], preferred_element_type=F32)
    c_q = pa[:, 0:Q_LORA]
    c_kv = pa[:, Q_LORA:Q_LORA + KV_LORA]
    o = Q_LORA + KV_LORA
    kr = pa[:, o:o + LANES]
    fl = pa[:, o + LANES:o + 2 * LANES]
    o += 2 * LANES
    fq = pa[:, o:o + HEADS * FOX_HD]
    fk = pa[:, o + HEADS * FOX_HD:o + 2 * HEADS * FOX_HD]

    cos = cos_ref[...]
    sin = sin_ref[...]

    cqn = _rms(c_q, qn_ref[...]).astype(BF16)
    q = jnp.dot(cqn, wq_ref[...], preferred_element_type=F32)
    ckvn = _rms(c_kv, kvn_ref[...]).astype(BF16)
    kb = jnp.dot(ckvn, wk_ref[...], preferred_element_type=F32)
    krr = _rope_tile(kr, cos, sin)
    q_scale = (MLA_NOPE + MLA_ROPE) ** -0.5 * LOG2E
    lane = lax.broadcasted_iota(jnp.int32, krr.shape, 1)
    for h in range(HEADS):
        sl = slice(h * HEAD_PAD, (h + 1) * HEAD_PAD)
        qm_ref[0, h] = (_rope_tile(q[:, sl], cos, sin) * q_scale).astype(BF16)
        kh = kb[:, (h // 2) * LANES:(h // 2 + 1) * LANES]
        if h % 2:
            kh = pltpu.roll(kh, LANES - MLA_NOPE, 1)
        km_ref[0, h] = jnp.where(lane < MLA_NOPE, kh, krr).astype(BF16)
    vt = lax.dot_general(wvt_ref[...], ckvn, _NT, preferred_element_type=F32)
    for h in range(HEADS):
        vm_ref[0, h, 0] = vt[h * MLA_V:(h + 1) * MLA_V, :].astype(BF16)

    z = fl + bf_ref[...]
    logf = jnp.minimum(z, 0.0) - jnp.log1p(jnp.exp(-jnp.abs(z)))
    r_i = lax.broadcasted_iota(jnp.int32, (tm, tm), 0)
    c_i = lax.broadcasted_iota(jnp.int32, (tm, tm), 1)
    tri = jnp.where(c_i <= r_i, 1.0, 0.0).astype(BF16)
    hi, mid, lo = _split3(logf)
    cs2 = jnp.dot(tri, jnp.concatenate([hi, mid], axis=1), preferred_element_type=F32)
    cs = cs2[:, :LANES] + cs2[:, LANES:] + jnp.dot(tri, lo, preferred_element_type=F32)
    c = carry_ref[0:1, :] + cs
    carry_ref[...] = jnp.broadcast_to(c[tm - 1:tm, :], carry_ref.shape)
    cl_ref[0] = c
    p_hi, p_mid, p_lo = (piece.astype(F32) for piece in _split3(c * LOG2E))
    cat = jnp.where(lane < HEADS, p_hi,
                    jnp.where(lane < 2 * HEADS, pltpu.roll(p_mid, HEADS, 1),
                              jnp.where(lane < 3 * HEADS, pltpu.roll(p_lo, 2 * HEADS, 1), 0.0))).astype(BF16)
    augq = jnp.dot(cat, eq_ref[...], preferred_element_type=F32) + oq_ref[...]
    augk = jnp.dot(cat, ek_ref[...], preferred_element_type=F32) + ok_ref[...]
    fqs = fq * (FOX_HD ** -0.5 * LOG2E)
    for h in range(HEADS):
        pair = slice((h // 2) * LANES, (h // 2 + 1) * LANES)
        sl = slice(h * HEAD_PAD, (h + 1) * HEAD_PAD)
        qh, kh = fqs[:, pair], fk[:, pair]
        if h % 2:
            qh, kh = pltpu.roll(qh, LANES - FOX_HD, 1), pltpu.roll(kh, LANES - FOX_HD, 1)
        qf_ref[0, h] = jnp.where(lane < FOX_HD, qh, augq[:, sl]).astype(BF16)
        kf_ref[0, h] = jnp.where(lane < FOX_HD, kh, augk[:, sl]).astype(BF16)
    vft = lax.dot_general(wfvt_ref[...], ub, _NT, preferred_element_type=F32)
    for h in range(HEADS):
        vf_ref[0, h, 0] = vft[h * FOX_HD:(h + 1) * FOX_HD, :].astype(BF16)


def _projections(x3, cos, sin, c0, consts, *, tm):
    b, l, _ = x3.shape
    nt = l // tm
    full = lambda a: pl.BlockSpec(a.shape, lambda bi, i: (0,) * a.ndim)
    qk_shape = jax.ShapeDtypeStruct((b, HEADS, l, HEAD_PAD), BF16)
    vt_shape = jax.ShapeDtypeStruct((b, HEADS, nt, MLA_V, tm), BF16)
    qk_spec = pl.BlockSpec((1, HEADS, tm, HEAD_PAD), lambda bi, i: (bi, 0, i, 0))
    vt_spec = pl.BlockSpec((1, HEADS, 1, MLA_V, tm), lambda bi, i: (bi, 0, i, 0, 0))
    return pl.pallas_call(
        functools.partial(_proj_kernel, tm=tm),
        grid=(b, nt),
        in_specs=[pl.BlockSpec((1, tm, D_MODEL), lambda bi, i: (bi, i, 0)),
                  pl.BlockSpec((tm, LANES), lambda bi, i: (i, 0)),
                  pl.BlockSpec((tm, LANES), lambda bi, i: (i, 0)),
                  full(c0)] + [full(a) for a in consts],
        out_specs=[qk_spec, qk_spec, vt_spec, qk_spec, qk_spec, vt_spec,
                   pl.BlockSpec((1, tm, LANES), lambda bi, i: (bi, i, 0))],
        out_shape=[qk_shape, qk_shape, vt_shape, qk_shape, qk_shape, vt_shape,
                   jax.ShapeDtypeStruct((b, l, LANES), F32)],
        scratch_shapes=[pltpu.VMEM((8, LANES), F32)],
        compiler_params=pltpu.CompilerParams(
            dimension_semantics=("arbitrary", "arbitrary"), vmem_limit_bytes=VMEM_LIMIT),
        name="proj",
    )(x3, cos, sin, c0, *consts)


def _attn_kernel(q_ref, k_ref, vt_ref, km_ref, vtm_ref, o_ref, m_ref, acc_ref, *, t, tk, g):
    i = pl.program_id(2)
    dv = vt_ref.shape[3]

    def ones_rows(n):
        return jnp.where(lax.broadcasted_iota(jnp.int32, (ACC_PAD, n), 0) == 0, 1.0, 0.0).astype(BF16)

    def update(hd, s, vt_blk, lo):
        m_prev = m_ref[hd, :, lo:]
        m_new = jnp.maximum(m_prev, jnp.max(s, axis=0, keepdims=True))
        alpha = jnp.exp2(m_prev - m_new)
        p = jnp.exp2(s - m_new).astype(BF16)
        vt_aug = jnp.concatenate([vt_blk, ones_rows(p.shape[0])], axis=0)
        pv = jnp.dot(vt_aug, p, preferred_element_type=F32)
        acc_ref[hd, :, lo:] = alpha * acc_ref[hd, :, lo:] + pv
        m_ref[hd, :, lo:] = m_new

    def run(items):
        def scores(n):
            hd, k_of, _, _, lo = items[n]
            return lax.dot_general(k_of(), q_ref[0, hd, lo:, :], _NT, preferred_element_type=F32)
        pending = [scores(n) for n in range(SKEW)]
        for n, (hd, _, vt_of, mask, lo) in enumerate(items):
            if n + SKEW < len(items):
                pending.append(scores(n + SKEW))
            s = pending.pop(0)
            if mask is not None:
                s = jnp.where(mask, s, NEG)
            update(hd, s, vt_of(), lo)

    def block_items(j, mask=None, lo=0):
        off = pl.multiple_of(j * tk, tk)
        return [(hd, lambda hd=hd: k_ref[0, hd, pl.ds(off, tk), :], lambda hd=hd: vt_ref[0, hd, j], mask, lo)
                for hd in range(g)]

    m_ref[...] = jnp.full(m_ref.shape, NEG, F32)
    acc_ref[...] = jnp.zeros(acc_ref.shape, F32)

    per = t // tk

    def body(jj, carry):
        run([it for d in range(per) for it in block_items(jj * per + d)])
        return carry

    lax.fori_loop(0, i, body, 0)

    tail = []
    for d in range(per):
        key_row = lax.broadcasted_iota(jnp.int32, (tk, t - d * tk), 0)
        query = lax.broadcasted_iota(jnp.int32, (tk, t - d * tk), 1)
        tail += block_items(i * per + d, key_row <= query, d * tk)
    row = lax.broadcasted_iota(jnp.int32, (META_PAD, t), 0)
    tail += [(hd, lambda hd=hd: km_ref[0, hd], lambda hd=hd: vtm_ref[0, hd, 0], row < N_META, 0)
             for hd in range(g)]
    run(tail)

    for hd in range(g):
        a = acc_ref[hd]
        o_ref[0, hd * dv:(hd + 1) * dv, :] = (a[:dv] / a[dv:dv + 1]).astype(o_ref.dtype)


def _attention(q, k, vt, k_meta, vt_meta, *, t, tk, g):
    b, h, l, _ = q.shape
    nq = l // t
    dv = vt.shape[3]
    return pl.pallas_call(
        functools.partial(_attn_kernel, t=t, tk=tk, g=g),
        grid=(b, h // g, nq),
        in_specs=[pl.BlockSpec((1, g, t, HEAD_PAD), lambda bi, hi, i: (bi, hi, i, 0)),
                  pl.BlockSpec((1, g, l, HEAD_PAD), lambda bi, hi, i: (bi, hi, 0, 0),
                               pipeline_mode=pl.Buffered(1)),
                  pl.BlockSpec((1, g, l // tk, dv, tk), lambda bi, hi, i: (bi, hi, 0, 0, 0),
                               pipeline_mode=pl.Buffered(1)),
                  pl.BlockSpec((1, g, META_PAD, HEAD_PAD), lambda bi, hi, i: (0, hi, 0, 0)),
                  pl.BlockSpec((1, g, 1, dv, META_PAD), lambda bi, hi, i: (0, hi, 0, 0, 0))],
        out_specs=pl.BlockSpec((1, g * dv, t), lambda bi, hi, i: (bi, hi, i)),
        out_shape=jax.ShapeDtypeStruct((b, h * dv, l), BF16),
        scratch_shapes=[pltpu.VMEM((g, 1, t), F32), pltpu.VMEM((g, dv + ACC_PAD, t), F32)],
        compiler_params=pltpu.CompilerParams(
            dimension_semantics=("arbitrary", "arbitrary", "arbitrary"), vmem_limit_bytes=VMEM_LIMIT),
        name="attn",
    )(q, k, vt, k_meta, vt_meta)


def _merge_kernel(x_ref, ya_ref, yb_ref, an_ref, wg_ref, wmo_ref, wfo_ref, wo_ref, fn_ref,
                  wrhl_ref, br_ref, h1_ref, t_ref, ri_ref, rw_ref, cnt_ref):
    @pl.when((pl.program_id(0) == 0) & (pl.program_id(1) == 0))
    def _():
        cnt_ref[...] = jnp.zeros_like(cnt_ref)

    x = x_ref[0]
    ub = _rms(x, an_ref[...]).astype(BF16)
    g = jnp.dot(ub, wg_ref[...], preferred_element_type=F32)
    ya = lax.dot_general(ya_ref[0], wmo_ref[...], _TN, preferred_element_type=F32)
    yb = lax.dot_general(yb_ref[0], wfo_ref[...], _TN, preferred_element_type=F32)
    merged = jax.nn.sigmoid(g[:, :D_MODEL]) * ya + jax.nn.sigmoid(g[:, D_MODEL:]) * yb
    h1 = x + jnp.dot(merged.astype(BF16), wo_ref[...], preferred_element_type=F32)
    h1_ref[0] = h1
    t = _rms(h1, fn_ref[...])
    _rows_to_tiles(t_ref, t, ROW_PITCH)

    th = t.astype(BF16)
    tl = (t - th.astype(F32)).astype(BF16)
    hl = jnp.dot(th, wrhl_ref[...], preferred_element_type=F32)
    logits = (hl[:, :LANES] + hl[:, LANES:]
              + jnp.dot(tl, wrhl_ref[:, :LANES], preferred_element_type=F32)) + br_ref[...]
    lane = lax.broadcasted_iota(jnp.int32, logits.shape, 1)
    big = jnp.int32(LANES)

    def first_argmax(v, vmax):
        return jnp.min(jnp.where(v == vmax, lane, big), axis=-1, keepdims=True)

    gmask = (lane >= N_EXPERTS) & (lane < N_EXPERTS + N_GROUPS)
    gl = jnp.where(gmask, logits, NEG)
    gmax = jnp.max(gl, axis=-1, keepdims=True)
    g_w = 1.0 / jnp.sum(jnp.where(gmask, jnp.exp(gl - gmax), 0.0), axis=-1, keepdims=True)
    g_idx = first_argmax(gl, gmax) - N_EXPERTS
    emask = (lane < N_EXPERTS) & (jnp.right_shift(lane, 3) == g_idx)
    el = jnp.where(emask, logits, NEG)
    m1 = jnp.max(el, axis=-1, keepdims=True)
    i1 = first_argmax(el, m1)
    el2 = jnp.where(lane == i1, NEG, el)
    m2 = jnp.max(el2, axis=-1, keepdims=True)
    i2 = first_argmax(el2, m2)
    zsum = jnp.sum(jnp.where(emask, jnp.exp(el - m1), 0.0), axis=-1, keepdims=True)
    p1 = 1.0 / zsum
    p2 = jnp.exp(m2 - m1) / zsum
    tot = p1 + p2
    w1 = g_w * (p1 / tot)
    w2 = g_w * (p2 / tot)
    rw_ref[0] = jnp.where(lane == 0, w1, jnp.where(lane == 1, w2, 0.0))

    tm = logits.shape[0]
    hot1 = lane == i1
    hot2 = lane == i2
    onehot = jnp.where(hot1, 1.0, jnp.where(hot2, 1.0, 0.0))
    r_i = lax.broadcasted_iota(jnp.int32, (tm, tm), 0)
    c_i = lax.broadcasted_iota(jnp.int32, (tm, tm), 1)
    before = jnp.where(c_i < r_i, 1.0, 0.0).astype(BF16)
    rank = cnt_ref[0:1, :] + jnp.dot(before, onehot.astype(BF16), preferred_element_type=F32)
    rank1 = jnp.sum(jnp.where(hot1, rank, 0.0), axis=-1, keepdims=True).astype(jnp.int32)
    rank2 = jnp.sum(jnp.where(hot2, rank, 0.0), axis=-1, keepdims=True).astype(jnp.int32)
    fields = jnp.where(lane == 0, i1, jnp.where(lane == 1, i2,
                       jnp.where(lane == 2, rank1, jnp.where(lane == 3, rank2, 0))))
    ri_ref[0] = fields.T[0:IDX_FIELDS, :]
    cnt_ref[...] = cnt_ref[...] + jnp.sum(onehot, axis=0, keepdims=True)


def _merge(x3, ya_t, yb_t, consts, *, tm):
    b, l, _ = x3.shape
    full = lambda a: pl.BlockSpec(a.shape, lambda bi, i: (0,) * a.ndim)
    tok = lambda w: pl.BlockSpec((1, tm, w), lambda bi, i: (bi, i, 0))
    yspec = pl.BlockSpec((1, ya_t.shape[1], tm), lambda bi, i: (bi, 0, i))
    return pl.pallas_call(
        _merge_kernel,
        grid=(b, l // tm),
        in_specs=[tok(D_MODEL), yspec, yspec] + [full(a) for a in consts],
        out_specs=[tok(D_MODEL),
                   pl.BlockSpec((tm * ROW_PITCH, LANES), lambda bi, i: (bi * (l // tm) + i, 0)),
                   pl.BlockSpec((1, IDX_FIELDS, tm), lambda bi, i: (bi * (l // tm) + i, 0, 0)), tok(LANES),
                   pl.BlockSpec((8, LANES), lambda bi, i: (0, 0))],
        out_shape=[jax.ShapeDtypeStruct((b, l, D_MODEL), F32),
                   jax.ShapeDtypeStruct((b * l * ROW_PITCH, LANES), F32),
                   jax.ShapeDtypeStruct((b * (l // tm), IDX_FIELDS, tm), jnp.int32),
                   jax.ShapeDtypeStruct((b, l, LANES), F32),
                   jax.ShapeDtypeStruct((8, LANES), F32)],
        compiler_params=pltpu.CompilerParams(
            dimension_semantics=("arbitrary", "arbitrary"), vmem_limit_bytes=VMEM_LIMIT),
        name="merge",
    )(x3, ya_t, yb_t, *consts)


def _dst_rows_kernel(rs_ref, idx_ref, d_ref):
    for k in range(2):
        e = idx_ref[:, k, :]
        start = jnp.zeros_like(e)
        for j in range(N_EXPERTS):
            start = jnp.where(e == j, rs_ref[j], start)
        d_ref[:, k, :] = start + idx_ref[:, 2 + k, :]


def _dst_rows(row_start, idx):
    nt, _, tm = idx.shape
    return pl.pallas_call(
        _dst_rows_kernel,
        grid_spec=pltpu.PrefetchScalarGridSpec(
            num_scalar_prefetch=1, grid=(1,),
            in_specs=[pl.BlockSpec(idx.shape, lambda i, rs: (0, 0, 0))],
            out_specs=pl.BlockSpec((nt, 2, tm), lambda i, rs: (0, 0, 0))),
        out_shape=jax.ShapeDtypeStruct((nt, 2, tm), jnp.int32),
        name="dst_rows",
    )(row_start, idx)


def _dispatch_kernel(rs_ref, cnt_ref, pad_ref, used_ref, d_ref, t_ref, xs_hbm, zero_ref, sem, zsem, *,
                     tm, te, nt):
    def row_tile(row):
        return xs_hbm.at[pl.ds(row * ROW_PITCH, ROW_PITCH)]

    def zero_row(row):
        return pltpu.make_async_copy(zero_ref.at[pl.ds(0, ROW_PITCH)], row_tile(row), zsem)

    def zero_tile(k):
        return pltpu.make_async_copy(zero_ref, xs_hbm.at[pl.ds(k * (te * ROW_PITCH), te * ROW_PITCH)], zsem)

    @pl.when(pl.program_id(0) == 0)
    def _():
        zero_ref[...] = jnp.zeros_like(zero_ref)
        for e in range(N_EXPERTS):
            def fill(p, carry, e=e):
                zero_row(rs_ref[e] + p).start()
                return carry
            lax.fori_loop(cnt_ref[e], pad_ref[e], fill, 0)

        def fill_tile(k, carry):
            zero_tile(k).start()
            return carry
        lax.fori_loop(used_ref[0], nt, fill_tile, 0)
        for e in range(N_EXPERTS):
            def drain(p, carry):
                zero_row(0).wait()
                return carry
            lax.fori_loop(cnt_ref[e], pad_ref[e], drain, 0)

        def drain_tile(k, carry):
            zero_tile(0).wait()
            return carry
        lax.fori_loop(used_ref[0], nt, drain_tile, 0)

    def copy(r, k):
        return pltpu.make_async_copy(t_ref.at[pl.ds(r * ROW_PITCH, ROW_PITCH)], row_tile(d_ref[0, k, r]), sem)

    def row(r, carry):
        copy(r, 0).start()
        copy(r, 1).start(priority=1)
        return carry
    lax.fori_loop(0, tm, row, 0, unroll=8)
    for _ in range(2 * tm):
        copy(0, 0).wait()


def _dispatch(row_start, counts, padded, tiles_used, dst, t_rows, n_tiles, *, tm, te):
    grid_spec = pltpu.PrefetchScalarGridSpec(
        num_scalar_prefetch=4,
        grid=(dst.shape[0],),
        in_specs=[pl.BlockSpec((1, 2, tm), lambda i, *_: (i, 0, 0), memory_space=pltpu.SMEM),
                  pl.BlockSpec((tm * ROW_PITCH, LANES), lambda i, *_: (i, 0))],
        out_specs=pl.BlockSpec(memory_space=pl.ANY),
        scratch_shapes=[pltpu.VMEM((te * ROW_PITCH, LANES), F32), pltpu.SemaphoreType.DMA(()),
                        pltpu.SemaphoreType.DMA(())])
    return pl.pallas_call(
        functools.partial(_dispatch_kernel, tm=tm, te=te, nt=n_tiles),
        grid_spec=grid_spec,
        out_shape=jax.ShapeDtypeStruct((n_tiles * te * ROW_PITCH, LANES), F32),
        compiler_params=pltpu.CompilerParams(dimension_semantics=("arbitrary",), vmem_limit_bytes=VMEM_LIMIT),
        name="dispatch",
    )(row_start, counts, padded, tiles_used, dst, t_rows)


def _expert_kernel(te_ref, tv_ref, xs_hbm, wg_ref, wu_ref, wd_ref, y_ref, xbuf, sem, *, tm, nt):
    del te_ref
    i = pl.program_id(0)
    rows = tm * ROW_PITCH

    def fetch(k):
        slot = lax.rem(k, X_AHEAD + 1)
        return pltpu.make_async_copy(xs_hbm.at[pl.ds(k * rows, rows)], xbuf.at[slot], sem.at[slot])

    @pl.when(i == 0)
    def _():
        for k in range(X_AHEAD):
            @pl.when(tv_ref[k] > 0)
            def _():
                fetch(k).start()

    ahead = jnp.minimum(i + X_AHEAD, nt - 1)

    @pl.when((i + X_AHEAD < nt) & (tv_ref[ahead] > 0))
    def _():
        fetch(ahead).start()

    @pl.when(tv_ref[i] > 0)
    def _():
        fetch(i).wait()
        x = _rows_from_tiles(xbuf.at[lax.rem(i, X_AHEAD + 1)], 0, tm, ROW_PITCH).astype(BF16)
        a = jnp.dot(x, wg_ref[0].astype(BF16), preferred_element_type=F32)
        u = jnp.dot(x, wu_ref[0].astype(BF16), preferred_element_type=F32)
        hm = (a * jax.nn.sigmoid(a)) * u
        _rows_to_tiles(y_ref, jnp.dot(hm.astype(BF16), wd_ref[0].astype(BF16), preferred_element_type=F32))

    @pl.when(tv_ref[i] == 0)
    def _():
        y_ref[...] = jnp.zeros_like(y_ref)


def _experts(tile_expert, tile_valid, xs, w_gate, w_up, w_down, *, tm):
    nt = tile_expert.shape[0]
    wspec = lambda shape: pl.BlockSpec((1,) + shape, lambda i, te, tv: (te[i], 0, 0))
    grid_spec = pltpu.PrefetchScalarGridSpec(
        num_scalar_prefetch=2,
        grid=(nt,),
        in_specs=[pl.BlockSpec(memory_space=pl.ANY),
                  wspec((D_MODEL, D_EXPERT)), wspec((D_MODEL, D_EXPERT)), wspec((D_EXPERT, D_MODEL))],
        out_specs=pl.BlockSpec((tm * ROW_TILE, LANES), lambda i, te, tv: (i, 0)),
        scratch_shapes=[pltpu.VMEM((X_AHEAD + 1, tm * ROW_PITCH, LANES), F32),
                        pltpu.SemaphoreType.DMA((X_AHEAD + 1,))])
    return pl.pallas_call(
        functools.partial(_expert_kernel, tm=tm, nt=nt),
        grid_spec=grid_spec,
        out_shape=jax.ShapeDtypeStruct((nt * tm * ROW_TILE, LANES), F32),
        compiler_params=pltpu.CompilerParams(dimension_semantics=("arbitrary",), vmem_limit_bytes=VMEM_LIMIT),
        name="experts",
    )(tile_expert, tile_valid, xs, w_gate, w_up, w_down)


def _combine_kernel(d_ref, d_next_ref, y_hbm, h1_ref, rw_ref, fn_ref, o_ref, ybuf, sem, *, tc, nt):
    i = pl.program_id(0)
    slot = lax.rem(i, 2)
    half = 2 * tc * ROW_PITCH

    def copy(dr, first, s, r, k):
        return pltpu.make_async_copy(y_hbm.at[dr[0, k, r]],
                                     ybuf.at[pl.ds(first + (k * tc + r) * ROW_PITCH, ROW_TILE)], sem.at[s])

    def gather(dr, first, s):
        def row(r, carry):
            copy(dr, first, s, r, 0).start()
            copy(dr, first, s, r, 1).start(priority=1)
            return carry
        lax.fori_loop(0, tc, row, 0, unroll=8)

    @pl.when(i == 0)
    def _():
        gather(d_ref, 0, 0)

    @pl.when(i + 1 < nt)
    def _():
        gather(d_next_ref, (1 - slot) * half, 1 - slot)

    for _ in range(2 * tc):
        copy(d_ref, slot * half, slot, 0, 0).wait()
    rw = rw_ref[...]
    y = (rw[:, 0:1] * _rows_from_tiles(ybuf, slot * half, tc, ROW_PITCH)
         + rw[:, 1:2] * _rows_from_tiles(ybuf, slot * half + tc * ROW_PITCH, tc, ROW_PITCH))
    o_ref[...] = _rms(h1_ref[...] + y, fn_ref[...])


def _combine(dst, y_rows, h2, rw2, final_norm, *, tc):
    n = h2.shape[0]
    nt = n // tc
    per = dst.shape[2] // tc
    blk = lambda i: (i // per, 0, lax.rem(i, per))
    return pl.pallas_call(
        functools.partial(_combine_kernel, tc=tc, nt=nt),
        grid=(nt,),
        in_specs=[pl.BlockSpec((1, 2, tc), lambda i: blk(i), memory_space=pltpu.SMEM),
                  pl.BlockSpec((1, 2, tc), lambda i: blk(jnp.minimum(i + 1, nt - 1)), memory_space=pltpu.SMEM),
                  pl.BlockSpec(memory_space=pl.ANY),
                  pl.BlockSpec((tc, D_MODEL), lambda i: (i, 0)),
                  pl.BlockSpec((tc, LANES), lambda i: (i, 0)),
                  pl.BlockSpec((1, D_MODEL), lambda i: (0, 0))],
        out_specs=pl.BlockSpec((tc, D_MODEL), lambda i: (i, 0)),
        out_shape=jax.ShapeDtypeStruct((n, D_MODEL), F32),
        scratch_shapes=[pltpu.VMEM((2 * 2 * tc * ROW_PITCH, LANES), F32), pltpu.SemaphoreType.DMA((2,))],
        compiler_params=pltpu.CompilerParams(dimension_semantics=("arbitrary",), vmem_limit_bytes=VMEM_LIMIT),
        name="combine",
    )(dst, dst, y_rows, h2, rw2, final_norm)


def _tile_plan(counts, n, *, tm):
    nt = (2 * n) // tm + N_EXPERTS
    tiles_e = (counts + tm - 1) // tm
    tile_end = jnp.cumsum(tiles_e)
    tile_start = tile_end - tiles_e
    k = jnp.arange(nt, dtype=jnp.int32)
    tile_expert = jnp.minimum(jnp.sum((k[:, None] >= tile_end[None, :]).astype(jnp.int32), axis=1), N_EXPERTS - 1)
    onehot = (tile_expert[:, None] == jnp.arange(N_EXPERTS)[None, :]).astype(jnp.int32)
    left = jnp.sum(onehot * (counts[None, :] - (k[:, None] - tile_start[None, :]) * tm), axis=1)
    tile_valid = jnp.where(k < tile_end[-1], jnp.clip(left, 0, tm), 0).astype(jnp.int32)
    return ((tile_start * tm).astype(jnp.int32), (tiles_e * tm).astype(jnp.int32),
            tile_end[-1:].astype(jnp.int32), tile_expert.astype(jnp.int32), tile_valid, nt)


def _pad_heads(w, width):
    k = w.shape[0]
    return jnp.pad(w.reshape(k, HEADS, width), ((0, 0), (0, 0), (0, HEAD_PAD - width))).reshape(k, HEADS * HEAD_PAD)


def _rope_tables(length):
    pos = np.arange(length, dtype=np.float64)
    inv = ROPE_THETA ** (-np.arange(0, MLA_ROPE, 2, dtype=np.float64) / MLA_ROPE)
    ang = pos[:, None] * inv[None, :]
    cos, sin = np.cos(ang), np.sin(ang)
    one = np.ones((length, ROPE_LO))
    zero_lo = np.zeros((length, ROPE_LO))
    zero_hi = np.zeros((length, LANES - ROPE_LO - MLA_ROPE))
    return (np.concatenate([one, cos, cos, zero_hi], axis=1).astype(np.float32),
            np.concatenate([zero_lo, -sin, sin, zero_hi], axis=1).astype(np.float32))


def _decay_columns():
    eq = np.zeros((LANES, HEADS * HEAD_PAD), np.float32)
    ek = np.zeros_like(eq)
    oq = np.zeros((1, HEADS * HEAD_PAD), np.float32)
    ok = np.zeros_like(oq)
    for h in range(HEADS):
        for p in range(N_SPLIT):
            eq[p * HEADS + h, h * HEAD_PAD + FOX_AUG + p] = 1.0
            ek[p * HEADS + h, h * HEAD_PAD + FOX_AUG + N_SPLIT + p] = -1.0
            oq[0, h * HEAD_PAD + FOX_AUG + N_SPLIT + p] = 1.0
            ok[0, h * HEAD_PAD + FOX_AUG + p] = 1.0
    return jnp.asarray(eq, BF16), jnp.asarray(ek, BF16), jnp.asarray(oq), jnp.asarray(ok)


def kernel(x, meta, attn_norm, w_in, b_forget, q_a_norm, w_q_up, kv_a_norm, w_kv_up, w_mla_out, w_fox_out, w_out, ffn_norm, w_group_router, b_group_router, w_expert_router, b_expert_router, w_gate, w_up, w_down, final_norm):
    b, seq, d = x.shape
    t_attn = 512
    row = lambda v: v.reshape(1, -1).astype(F32)

    w = w_in[0]
    offs = np.cumsum([0, Q_LORA, KV_LORA, MLA_ROPE, HEADS * FOX_HD, HEADS * FOX_HD, HEADS * FOX_HD,
                      HEADS, D_MODEL, D_MODEL])
    seg = lambda j: w[:, offs[j]:offs[j + 1]]
    zcol = lambda n: jnp.zeros((d, n), F32)
    w_a = jnp.concatenate(
        [seg(0), seg(1),
         zcol(ROPE_LO), seg(2), zcol(LANES - ROPE_LO - MLA_ROPE),
         seg(6), zcol(LANES - HEADS),
         seg(3), seg(4)], axis=1).astype(BF16)
    w_fvt = seg(5).T.astype(BF16)
    w_gates = jnp.concatenate([seg(7), seg(8)], axis=1).astype(BF16)
    w_q = _pad_heads(w_q_up[0], MLA_NOPE + MLA_ROPE).astype(BF16)
    kv3 = w_kv_up[0].reshape(KV_LORA, HEADS, MLA_NOPE + MLA_V)
    w_k = kv3[:, :, :MLA_NOPE].reshape(KV_LORA, HEADS * MLA_NOPE).astype(BF16)
    w_vt = kv3[:, :, MLA_NOPE:].reshape(KV_LORA, HEADS * MLA_V).T.astype(BF16)
    bf128 = jnp.pad(row(b_forget[0]), ((0, 0), (0, LANES - HEADS)))
    eq, ek, oq, ok = _decay_columns()
    cos, sin = _rope_tables(N_META + seq)
    consts = [row(attn_norm[0]), w_a, row(q_a_norm[0]), w_q, row(kv_a_norm[0]), w_k, w_vt, w_fvt, bf128,
              eq, ek, oq, ok]

    meta_pad = jnp.pad(meta.astype(F32), ((0, META_PAD - N_META), (0, 0)))[None]
    pad_tab = lambda tab: np.pad(tab[:N_META], ((0, META_PAD - N_META), (0, 0)))
    _, km_meta, vm_meta, _, kf_meta, vf_meta, c_meta = _projections(
        meta_pad, pad_tab(cos), pad_tab(sin), jnp.zeros((1, LANES), F32), consts, tm=META_PAD)
    c0 = c_meta[0, N_META - 1:N_META, :]

    qm, km, vm, qf, kf, vf, _ = _projections(x, cos[N_META:], sin[N_META:], c0, consts, tm=t_attn)
    ya_t = _attention(qm, km, vm, km_meta, vm_meta, t=ATTN_QUERIES, tk=t_attn, g=ATTN_HEADS_PER_STEP)
    yb_t = _attention(qf, kf, vf, kf_meta, vf_meta, t=ATTN_QUERIES, tk=t_attn, g=ATTN_HEADS_PER_STEP)

    w_r = jnp.concatenate([w_expert_router[0], w_group_router[0],
                           jnp.zeros((d, LANES - N_EXPERTS - N_GROUPS), F32)], axis=1)
    w_rh = w_r.astype(BF16)
    w_rhl = jnp.concatenate([w_rh, (w_r - w_rh.astype(F32)).astype(BF16)], axis=1)
    b_r = jnp.pad(jnp.concatenate([row(b_expert_router[0]), row(b_group_router[0])], axis=1),
                  ((0, 0), (0, LANES - N_EXPERTS - N_GROUPS)))
    h1, t_rows, ri, rw, cnt = _merge(
        x, ya_t, yb_t,
        [row(attn_norm[0]), w_gates, w_mla_out[0].astype(BF16), w_fox_out[0].astype(BF16),
         w_out[0].astype(BF16), row(ffn_norm[0]), w_rhl, b_r], tm=512)

    n = b * seq
    counts = cnt[0, :N_EXPERTS].astype(jnp.int32)
    row_start, padded, tiles_used, tile_expert, tile_valid, nt = _tile_plan(counts, n, tm=EXPERT_ROWS)
    dst = _dst_rows(row_start, ri)
    xs = _dispatch(row_start, counts, padded, tiles_used, dst, t_rows, nt, tm=512, te=EXPERT_ROWS)
    y_rows = _experts(tile_expert, tile_valid, xs,
                      w_gate[0].reshape(N_EXPERTS, d, D_EXPERT), w_up[0].reshape(N_EXPERTS, d, D_EXPERT),
                      w_down[0].reshape(N_EXPERTS, D_EXPERT, d), tm=EXPERT_ROWS)
    out = _combine(dst, y_rows.reshape(-1, ROW_TILE, LANES), h1.reshape(n, d), rw.reshape(n, LANES),
                   row(final_norm), tc=COMBINE_TOKENS)
    return out.reshape(b, seq, d)
```

```python
import functools

import numpy as np
import jax
import jax.numpy as jnp
from jax import lax
from jax.experimental import pallas as pl
from jax.experimental.pallas import tpu as pltpu

F32 = jnp.float32
BF16 = jnp.bfloat16

D_MODEL = 1024
N_META = 16
HEADS = 8
MLA_NOPE, MLA_ROPE, MLA_V = 64, 32, 64
Q_LORA, KV_LORA = 512, 256
ROPE_THETA = 10000.0
FOX_HD = 64
N_GROUPS, EXPERTS_PER_GROUP, D_EXPERT = 4, 8, 256
N_EXPERTS = N_GROUPS * EXPERTS_PER_GROUP
EPS = 1e-6
NEG = -1e30
LOG2E = 1.4426950408889634

LANES = 128
HEAD_PAD = LANES
META_PAD = LANES
ROPE_LO = MLA_NOPE
ROPE_HALF = MLA_ROPE // 2
FOX_AUG = FOX_HD
N_SPLIT = 3

VMEM_LIMIT = 56 * 1024 * 1024
ATTN_QUERIES = 1024
ATTN_HEADS_PER_STEP = 8
SKEW = 2
ACC_PAD = 16
ROW_TILE = D_MODEL // LANES
ROW_PITCH = ROW_TILE + 1
IDX_FIELDS = 4
X_AHEAD = 2
EXPERT_ROWS = 256
COMBINE_TOKENS = 256

_NT = (((1,), (1,)), ((), ()))
_TN = (((0,), (0,)), ((), ()))
_TT = (((0,), (1,)), ((), ()))


def _rms(x, g):
    return x * lax.rsqrt(jnp.mean(x * x, axis=-1, keepdims=True) + EPS) * g


def _rows_to_tiles(ref2d, v, pitch=ROW_TILE):
    n = v.shape[0]
    for c in range(ROW_TILE):
        ref2d[pl.ds(c, n, stride=pitch), :] = v[:, c * LANES:(c + 1) * LANES]
    for c in range(ROW_TILE, pitch):
        ref2d[pl.ds(c, n, stride=pitch), :] = jnp.zeros((n, LANES), v.dtype)


def _rows_from_tiles(ref2d, first, n, pitch):
    return jnp.concatenate([ref2d[pl.ds(first + c, n, stride=pitch), :] for c in range(ROW_TILE)], axis=1)


def _split3(v):
    hi = v.astype(BF16)
    r = v - hi.astype(F32)
    mid = r.astype(BF16)
    lo = (r - mid.astype(F32)).astype(BF16)
    return hi, mid, lo


def _rope_tile(v, c, s):
    lane = lax.broadcasted_iota(jnp.int32, v.shape, 1)
    partner = jnp.where(lane < ROPE_LO + ROPE_HALF,
                        pltpu.roll(v, LANES - ROPE_HALF, 1), pltpu.roll(v, ROPE_HALF, 1))
    return v * c + partner * s


def _proj_kernel(x_ref, cos_ref, sin_ref, c0_ref, an_ref, wa_ref, qn_ref, wq_ref, kvn_ref, wk_ref,
                 wv_ref, wfv_ref, bf_ref, eq_ref, ek_ref, oq_ref, ok_ref,
                 qm_ref, km_ref, vm_ref, qf_ref, kf_ref, vf_ref, cl_ref, carry_ref, *, tm):
    i = pl.program_id(1)

    @pl.when(i == 0)
    def _():
        carry_ref[...] = jnp.broadcast_to(c0_ref[...], carry_ref.shape)

    ub = _rms(x_ref[0], an_ref[...]).astype(BF16)
    pa = jnp.dot(ub, wa_ref[...], preferred_element_type=F32)
    c_q = pa[:, 0:Q_LORA]
    c_kv = pa[:, Q_LORA:Q_LORA + KV_LORA]
    o = Q_LORA + KV_LORA
    kr = pa[:, o:o + LANES]
    fl = pa[:, o + LANES:o + 2 * LANES]
    o += 2 * LANES
    fq = pa[:, o:o + HEADS * FOX_HD]
    fk = pa[:, o + HEADS * FOX_HD:o + 2 * HEADS * FOX_HD]

    cos = cos_ref[...]
    sin = sin_ref[...]

    cqn = _rms(c_q, qn_ref[...]).astype(BF16)
    q = jnp.dot(cqn, wq_ref[...], preferred_element_type=F32)
    ckvn = _rms(c_kv, kvn_ref[...]).astype(BF16)
    kb = jnp.dot(ckvn, wk_ref[...], preferred_element_type=F32)
    krr = _rope_tile(kr, cos, sin)
    q_scale = (MLA_NOPE + MLA_ROPE) ** -0.5 * LOG2E
    lane = lax.broadcasted_iota(jnp.int32, krr.shape, 1)
    for h in range(HEADS):
        sl = slice(h * HEAD_PAD, (h + 1) * HEAD_PAD)
        qm_ref[0, h] = (_rope_tile(q[:, sl], cos, sin) * q_scale).astype(BF16)
        kh = kb[:, (h // 2) * LANES:(h // 2 + 1) * LANES]
        if h % 2:
            kh = pltpu.roll(kh, LANES - MLA_NOPE, 1)
        km_ref[0, h] = jnp.where(lane < MLA_NOPE, kh, krr).astype(BF16)
    vt = lax.dot_general(wv_ref[...], ckvn, _TT, preferred_element_type=F32)
    for h in range(HEADS):
        vm_ref[0, h, 0] = vt[h * MLA_V:(h + 1) * MLA_V, :].astype(BF16)

    z = fl + bf_ref[...]
    logf = jnp.minimum(z, 0.0) - jnp.log1p(jnp.exp(-jnp.abs(z)))
    r_i = lax.broadcasted_iota(jnp.int32, (tm, tm), 0)
    c_i = lax.broadcasted_iota(jnp.int32, (tm, tm), 1)
    tri = jnp.where(c_i <= r_i, 1.0, 0.0).astype(BF16)
    hi, mid, lo = _split3(logf)
    cs2 = jnp.dot(tri, jnp.concatenate([hi, mid], axis=1), preferred_element_type=F32)
    cs = cs2[:, :LANES] + cs2[:, LANES:] + jnp.dot(tri, lo, preferred_element_type=F32)
    c = carry_ref[0:1, :] + cs
    carry_ref[...] = jnp.broadcast_to(c[tm - 1:tm, :], carry_ref.shape)
    cl_ref[0] = c
    p_hi, p_mid, p_lo = (piece.astype(F32) for piece in _split3(c * LOG2E))
    cat = jnp.where(lane < HEADS, p_hi,
                    jnp.where(lane < 2 * HEADS, pltpu.roll(p_mid, HEADS, 1),
                              jnp.where(lane < 3 * HEADS, pltpu.roll(p_lo, 2 * HEADS, 1), 0.0))).astype(BF16)
    augq = jnp.dot(cat, eq_ref[...], preferred_element_type=F32) + oq_ref[...]
    augk = jnp.dot(cat, ek_ref[...], preferred_element_type=F32) + ok_ref[...]
    fqs = fq * (FOX_HD ** -0.5 * LOG2E)
    for h in range(HEADS):
        pair = slice((h // 2) * LANES, (h // 2 + 1) * LANES)
        sl = slice(h * HEAD_PAD, (h + 1) * HEAD_PAD)
        qh, kh = fqs[:, pair], fk[:, pair]
        if h % 2:
            qh, kh = pltpu.roll(qh, LANES - FOX_HD, 1), pltpu.roll(kh, LANES - FOX_HD, 1)
        qf_ref[0, h] = jnp.where(lane < FOX_HD, qh, augq[:, sl]).astype(BF16)
        kf_ref[0, h] = jnp.where(lane < FOX_HD, kh, augk[:, sl]).astype(BF16)
    vft = lax.dot_general(wfv_ref[...], ub, _TT, preferred_element_type=F32)
    for h in range(HEADS):
        vf_ref[0, h, 0] = vft[h * FOX_HD:(h + 1) * FOX_HD, :].astype(BF16)


def _projections(x3, cos, sin, c0, consts, *, tm):
    b, l, _ = x3.shape
    nt = l // tm
    full = lambda a: pl.BlockSpec(a.shape, lambda bi, i: (0,) * a.ndim)
    qk_shape = jax.ShapeDtypeStruct((b, HEADS, l, HEAD_PAD), BF16)
    vt_shape = jax.ShapeDtypeStruct((b, HEADS, nt, MLA_V, tm), BF16)
    qk_spec = pl.BlockSpec((1, HEADS, tm, HEAD_PAD), lambda bi, i: (bi, 0, i, 0))
    vt_spec = pl.BlockSpec((1, HEADS, 1, MLA_V, tm), lambda bi, i: (bi, 0, i, 0, 0))
    return pl.pallas_call(
        functools.partial(_proj_kernel, tm=tm),
        grid=(b, nt),
        in_specs=[pl.BlockSpec((1, tm, D_MODEL), lambda bi, i: (bi, i, 0)),
                  pl.BlockSpec((tm, LANES), lambda bi, i: (i, 0)),
                  pl.BlockSpec((tm, LANES), lambda bi, i: (i, 0)),
                  full(c0)] + [full(a) for a in consts],
        out_specs=[qk_spec, qk_spec, vt_spec, qk_spec, qk_spec, vt_spec,
                   pl.BlockSpec((1, tm, LANES), lambda bi, i: (bi, i, 0))],
        out_shape=[qk_shape, qk_shape, vt_shape, qk_shape, qk_shape, vt_shape,
                   jax.ShapeDtypeStruct((b, l, LANES), F32)],
        scratch_shapes=[pltpu.VMEM((8, LANES), F32)],
        compiler_params=pltpu.CompilerParams(
            dimension_semantics=("arbitrary", "arbitrary"), vmem_limit_bytes=VMEM_LIMIT),
        name="proj",
    )(x3, cos, sin, c0, *consts)


def _attn_kernel(q_ref, k_ref, vt_ref, km_ref, vtm_ref, o_ref, m_ref, acc_ref, *, t, tk, g):
    i = pl.program_id(2)
    dv = vt_ref.shape[3]

    def ones_rows(n):
        return jnp.where(lax.broadcasted_iota(jnp.int32, (ACC_PAD, n), 0) == 0, 1.0, 0.0).astype(BF16)

    def update(hd, s, vt_blk, lo):
        m_prev = m_ref[hd, :, lo:]
        m_new = jnp.maximum(m_prev, jnp.max(s, axis=0, keepdims=True))
        alpha = jnp.exp2(m_prev - m_new)
        p = jnp.exp2(s - m_new).astype(BF16)
        vt_aug = jnp.concatenate([vt_blk, ones_rows(p.shape[0])], axis=0)
        pv = jnp.dot(vt_aug, p, preferred_element_type=F32)
        acc_ref[hd, :, lo:] = alpha * acc_ref[hd, :, lo:] + pv
        m_ref[hd, :, lo:] = m_new

    def run(items):
        def scores(n):
            hd, k_of, _, _, lo = items[n]
            return lax.dot_general(k_of(), q_ref[0, hd, lo:, :], _NT, preferred_element_type=F32)
        pending = [scores(n) for n in range(SKEW)]
        for n, (hd, _, vt_of, mask, lo) in enumerate(items):
            if n + SKEW < len(items):
                pending.append(scores(n + SKEW))
            s = pending.pop(0)
            if mask is not None:
                s = jnp.where(mask, s, NEG)
            update(hd, s, vt_of(), lo)

    def block_items(j, mask=None, lo=0):
        off = pl.multiple_of(j * tk, tk)
        return [(hd, lambda hd=hd: k_ref[0, hd, pl.ds(off, tk), :], lambda hd=hd: vt_ref[0, hd, j], mask, lo)
                for hd in range(g)]

    m_ref[...] = jnp.full(m_ref.shape, NEG, F32)
    acc_ref[...] = jnp.zeros(acc_ref.shape, F32)

    per = t // tk

    def body(jj, carry):
        run([it for d in range(per) for it in block_items(jj * per + d)])
        return carry

    lax.fori_loop(0, i, body, 0)

    tail = []
    for d in range(per):
        key_row = lax.broadcasted_iota(jnp.int32, (tk, t - d * tk), 0)
        query = lax.broadcasted_iota(jnp.int32, (tk, t - d * tk), 1)
        tail += block_items(i * per + d, key_row <= query, d * tk)
    row = lax.broadcasted_iota(jnp.int32, (META_PAD, t), 0)
    tail += [(hd, lambda hd=hd: km_ref[0, hd], lambda hd=hd: vtm_ref[0, hd, 0], row < N_META, 0)
             for hd in range(g)]
    run(tail)

    for hd in range(g):
        a = acc_ref[hd]
        o_ref[0, hd * dv:(hd + 1) * dv, :] = (a[:dv] / a[dv:dv + 1]).astype(o_ref.dtype)


def _attention(q, k, vt, k_meta, vt_meta, *, t, tk, g):
    b, h, l, _ = q.shape
    nq = l // t
    dv = vt.shape[3]
    return pl.pallas_call(
        functools.partial(_attn_kernel, t=t, tk=tk, g=g),
        grid=(b, h // g, nq),
        in_specs=[pl.BlockSpec((1, g, t, HEAD_PAD), lambda bi, hi, i: (bi, hi, i, 0)),
                  pl.BlockSpec((1, g, l, HEAD_PAD), lambda bi, hi, i: (bi, hi, 0, 0),
                               pipeline_mode=pl.Buffered(1)),
                  pl.BlockSpec((1, g, l // tk, dv, tk), lambda bi, hi, i: (bi, hi, 0, 0, 0),
                               pipeline_mode=pl.Buffered(1)),
                  pl.BlockSpec((1, g, META_PAD, HEAD_PAD), lambda bi, hi, i: (0, hi, 0, 0)),
                  pl.BlockSpec((1, g, 1, dv, META_PAD), lambda bi, hi, i: (0, hi, 0, 0, 0))],
        out_specs=pl.BlockSpec((1, g * dv, t), lambda bi, hi, i: (bi, hi, i)),
        out_shape=jax.ShapeDtypeStruct((b, h * dv, l), BF16),
        scratch_shapes=[pltpu.VMEM((g, 1, t), F32), pltpu.VMEM((g, dv + ACC_PAD, t), F32)],
        compiler_params=pltpu.CompilerParams(
            dimension_semantics=("arbitrary", "arbitrary", "arbitrary"), vmem_limit_bytes=VMEM_LIMIT),
        name="attn",
    )(q, k, vt, k_meta, vt_meta)


def _merge_kernel(x_ref, ya_ref, yb_ref, an_ref, wg_ref, wmo_ref, wfo_ref, wo_ref, fn_ref,
                  wrhl_ref, br_ref, h1_ref, t_ref, ri_ref, rw_ref, cnt_ref):
    @pl.when((pl.program_id(0) == 0) & (pl.program_id(1) == 0))
    def _():
        cnt_ref[...] = jnp.zeros_like(cnt_ref)

    x = x_ref[0]
    ub = _rms(x, an_ref[...]).astype(BF16)
    g = jnp.dot(ub, wg_ref[...], preferred_element_type=F32)
    ya = lax.dot_general(ya_ref[0], wmo_ref[...], _TN, preferred_element_type=F32)
    yb = lax.dot_general(yb_ref[0], wfo_ref[...], _TN, preferred_element_type=F32)
    merged = jax.nn.sigmoid(g[:, :D_MODEL]) * ya + jax.nn.sigmoid(g[:, D_MODEL:]) * yb
    h1 = x + jnp.dot(merged.astype(BF16), wo_ref[...], preferred_element_type=F32)
    h1_ref[0] = h1
    t = _rms(h1, fn_ref[...])
    _rows_to_tiles(t_ref, t, ROW_PITCH)

    th = t.astype(BF16)
    tl = (t - th.astype(F32)).astype(BF16)
    hl = jnp.dot(th, wrhl_ref[...], preferred_element_type=F32)
    logits = (hl[:, :LANES] + hl[:, LANES:]
              + jnp.dot(tl, wrhl_ref[:, :LANES], preferred_element_type=F32)) + br_ref[...]
    lane = lax.broadcasted_iota(jnp.int32, logits.shape, 1)
    big = jnp.int32(LANES)

    def first_argmax(v, vmax):
        return jnp.min(jnp.where(v == vmax, lane, big), axis=-1, keepdims=True)

    gmask = (lane >= N_EXPERTS) & (lane < N_EXPERTS + N_GROUPS)
    gl = jnp.where(gmask, logits, NEG)
    gmax = jnp.max(gl, axis=-1, keepdims=True)
    g_w = 1.0 / jnp.sum(jnp.where(gmask, jnp.exp(gl - gmax), 0.0), axis=-1, keepdims=True)
    g_idx = first_argmax(gl, gmax) - N_EXPERTS
    emask = (lane < N_EXPERTS) & (jnp.right_shift(lane, 3) == g_idx)
    el = jnp.where(emask, logits, NEG)
    m1 = jnp.max(el, axis=-1, keepdims=True)
    i1 = first_argmax(el, m1)
    el2 = jnp.where(lane == i1, NEG, el)
    m2 = jnp.max(el2, axis=-1, keepdims=True)
    i2 = first_argmax(el2, m2)
    zsum = jnp.sum(jnp.where(emask, jnp.exp(el - m1), 0.0), axis=-1, keepdims=True)
    p1 = 1.0 / zsum
    p2 = jnp.exp(m2 - m1) / zsum
    tot = p1 + p2
    w1 = g_w * (p1 / tot)
    w2 = g_w * (p2 / tot)
    rw_ref[0] = jnp.where(lane == 0, w1, jnp.where(lane == 1, w2, 0.0))

    tm = logits.shape[0]
    hot1 = lane == i1
    hot2 = lane == i2
    onehot = jnp.where(hot1, 1.0, jnp.where(hot2, 1.0, 0.0))
    r_i = lax.broadcasted_iota(jnp.int32, (tm, tm), 0)
    c_i = lax.broadcasted_iota(jnp.int32, (tm, tm), 1)
    before = jnp.where(c_i < r_i, 1.0, 0.0).astype(BF16)
    rank = cnt_ref[0:1, :] + jnp.dot(before, onehot.astype(BF16), preferred_element_type=F32)
    rank1 = jnp.sum(jnp.where(hot1, rank, 0.0), axis=-1, keepdims=True).astype(jnp.int32)
    rank2 = jnp.sum(jnp.where(hot2, rank, 0.0), axis=-1, keepdims=True).astype(jnp.int32)
    fields = jnp.where(lane == 0, i1, jnp.where(lane == 1, i2,
                       jnp.where(lane == 2, rank1, jnp.where(lane == 3, rank2, 0))))
    ri_ref[0] = fields.T[0:IDX_FIELDS, :]
    cnt_ref[...] = cnt_ref[...] + jnp.sum(onehot, axis=0, keepdims=True)


def _merge(x3, ya_t, yb_t, consts, *, tm):
    b, l, _ = x3.shape
    full = lambda a: pl.BlockSpec(a.shape, lambda bi, i: (0,) * a.ndim)
    tok = lambda w: pl.BlockSpec((1, tm, w), lambda bi, i: (bi, i, 0))
    yspec = pl.BlockSpec((1, ya_t.shape[1], tm), lambda bi, i: (bi, 0, i))
    return pl.pallas_call(
        _merge_kernel,
        grid=(b, l // tm),
        in_specs=[tok(D_MODEL), yspec, yspec] + [full(a) for a in consts],
        out_specs=[tok(D_MODEL),
                   pl.BlockSpec((tm * ROW_PITCH, LANES), lambda bi, i: (bi * (l // tm) + i, 0)),
                   pl.BlockSpec((1, IDX_FIELDS, tm), lambda bi, i: (bi * (l // tm) + i, 0, 0)), tok(LANES),
                   pl.BlockSpec((8, LANES), lambda bi, i: (0, 0))],
        out_shape=[jax.ShapeDtypeStruct((b, l, D_MODEL), F32),
                   jax.ShapeDtypeStruct((b * l * ROW_PITCH, LANES), F32),
                   jax.ShapeDtypeStruct((b * (l // tm), IDX_FIELDS, tm), jnp.int32),
                   jax.ShapeDtypeStruct((b, l, LANES), F32),
                   jax.ShapeDtypeStruct((8, LANES), F32)],
        compiler_params=pltpu.CompilerParams(
            dimension_semantics=("arbitrary", "arbitrary"), vmem_limit_bytes=VMEM_LIMIT),
        name="merge",
    )(x3, ya_t, yb_t, *consts)


def _dst_rows_kernel(rs_ref, idx_ref, d_ref):
    for k in range(2):
        e = idx_ref[:, k, :]
        start = jnp.zeros_like(e)
        for j in range(N_EXPERTS):
            start = jnp.where(e == j, rs_ref[j], start)
        d_ref[:, k, :] = start + idx_ref[:, 2 + k, :]


def _dst_rows(row_start, idx):
    nt, _, tm = idx.shape
    return pl.pallas_call(
        _dst_rows_kernel,
        grid_spec=pltpu.PrefetchScalarGridSpec(
            num_scalar_prefetch=1, grid=(1,),
            in_specs=[pl.BlockSpec(idx.shape, lambda i, rs: (0, 0, 0))],
            out_specs=pl.BlockSpec((nt, 2, tm), lambda i, rs: (0, 0, 0))),
        out_shape=jax.ShapeDtypeStruct((nt, 2, tm), jnp.int32),
        name="dst_rows",
    )(row_start, idx)


def _dispatch_kernel(rs_ref, cnt_ref, pad_ref, used_ref, d_ref, t_ref, xs_hbm, zero_ref, sem, zsem, *,
                     tm, te, nt):
    def row_tile(row):
        return xs_hbm.at[pl.ds(row * ROW_PITCH, ROW_PITCH)]

    def zero_rows(first, n):
        return pltpu.make_async_copy(zero_ref.at[pl.ds(0, n * ROW_PITCH)],
                                     xs_hbm.at[pl.ds(first * ROW_PITCH, n * ROW_PITCH)], zsem)

    def pad_copies(act):
        for e in range(N_EXPERTS):
            n_pad = pad_ref[e] - cnt_ref[e]
            first = rs_ref[e] + cnt_ref[e]
            size = te // 2
            while size:
                @pl.when((n_pad & size) != 0)
                def _(size=size, n_pad=n_pad, first=first):
                    act(zero_rows(first + (n_pad & ~(2 * size - 1)), size))
                size //= 2

    def tail_copies(act):
        def tile(k, carry):
            act(zero_rows(k * te, te))
            return carry
        lax.fori_loop(used_ref[0], nt, tile, 0)

    @pl.when(pl.program_id(0) == 0)
    def _():
        zero_ref[...] = jnp.zeros_like(zero_ref)
        pad_copies(lambda cp: cp.start())
        tail_copies(lambda cp: cp.start())
        pad_copies(lambda cp: cp.wait())
        tail_copies(lambda cp: cp.wait())

    def copy(r, k):
        return pltpu.make_async_copy(t_ref.at[pl.ds(r * ROW_PITCH, ROW_PITCH)], row_tile(d_ref[0, k, r]), sem)

    def row(r, carry):
        copy(r, 0).start()
        copy(r, 1).start(priority=1)
        return carry
    lax.fori_loop(0, tm, row, 0, unroll=8)
    for _ in range(2 * tm):
        copy(0, 0).wait()


def _dispatch(row_start, counts, padded, tiles_used, dst, t_rows, n_tiles, *, tm, te):
    grid_spec = pltpu.PrefetchScalarGridSpec(
        num_scalar_prefetch=4,
        grid=(dst.shape[0],),
        in_specs=[pl.BlockSpec((1, 2, tm), lambda i, *_: (i, 0, 0), memory_space=pltpu.SMEM),
                  pl.BlockSpec((tm * ROW_PITCH, LANES), lambda i, *_: (i, 0))],
        out_specs=pl.BlockSpec(memory_space=pl.ANY),
        scratch_shapes=[pltpu.VMEM((te * ROW_PITCH, LANES), F32), pltpu.SemaphoreType.DMA(()),
                        pltpu.SemaphoreType.DMA(())])
    return pl.pallas_call(
        functools.partial(_dispatch_kernel, tm=tm, te=te, nt=n_tiles),
        grid_spec=grid_spec,
        out_shape=jax.ShapeDtypeStruct((n_tiles * te * ROW_PITCH, LANES), F32),
        compiler_params=pltpu.CompilerParams(dimension_semantics=("arbitrary",), vmem_limit_bytes=VMEM_LIMIT),
        name="dispatch",
    )(row_start, counts, padded, tiles_used, dst, t_rows)


def _expert_kernel(te_ref, tv_ref, xs_hbm, wg_ref, wu_ref, wd_ref, y_ref, xbuf, sem, *, tm, nt):
    del te_ref
    i = pl.program_id(0)
    rows = tm * ROW_PITCH

    def fetch(k):
        slot = lax.rem(k, X_AHEAD + 1)
        return pltpu.make_async_copy(xs_hbm.at[pl.ds(k * rows, rows)], xbuf.at[slot], sem.at[slot])

    @pl.when(i == 0)
    def _():
        for k in range(X_AHEAD):
            @pl.when(tv_ref[k] > 0)
            def _():
                fetch(k).start()

    ahead = jnp.minimum(i + X_AHEAD, nt - 1)

    @pl.when((i + X_AHEAD < nt) & (tv_ref[ahead] > 0))
    def _():
        fetch(ahead).start()

    @pl.when(tv_ref[i] > 0)
    def _():
        fetch(i).wait()
        x = _rows_from_tiles(xbuf.at[lax.rem(i, X_AHEAD + 1)], 0, tm, ROW_PITCH).astype(BF16)
        a = jnp.dot(x, wg_ref[0].astype(BF16), preferred_element_type=F32)
        u = jnp.dot(x, wu_ref[0].astype(BF16), preferred_element_type=F32)
        hm = (a * jax.nn.sigmoid(a)) * u
        _rows_to_tiles(y_ref, jnp.dot(hm.astype(BF16), wd_ref[0].astype(BF16), preferred_element_type=F32))

    @pl.when(tv_ref[i] == 0)
    def _():
        y_ref[...] = jnp.zeros_like(y_ref)


def _experts(tile_expert, tile_valid, xs, w_gate, w_up, w_down, *, tm):
    nt = tile_expert.shape[0]
    wspec = lambda shape: pl.BlockSpec((1,) + shape, lambda i, te, tv: (te[i], 0, 0))
    grid_spec = pltpu.PrefetchScalarGridSpec(
        num_scalar_prefetch=2,
        grid=(nt,),
        in_specs=[pl.BlockSpec(memory_space=pl.ANY),
                  wspec((D_MODEL, D_EXPERT)), wspec((D_MODEL, D_EXPERT)), wspec((D_EXPERT, D_MODEL))],
        out_specs=pl.BlockSpec((tm * ROW_TILE, LANES), lambda i, te, tv: (i, 0)),
        scratch_shapes=[pltpu.VMEM((X_AHEAD + 1, tm * ROW_PITCH, LANES), F32),
                        pltpu.SemaphoreType.DMA((X_AHEAD + 1,))])
    return pl.pallas_call(
        functools.partial(_expert_kernel, tm=tm, nt=nt),
        grid_spec=grid_spec,
        out_shape=jax.ShapeDtypeStruct((nt * tm * ROW_TILE, LANES), F32),
        compiler_params=pltpu.CompilerParams(dimension_semantics=("arbitrary",), vmem_limit_bytes=VMEM_LIMIT),
        name="experts",
    )(tile_expert, tile_valid, xs, w_gate, w_up, w_down)


def _combine_kernel(d_ref, d_next_ref, y_hbm, h1_ref, rw_ref, fn_ref, o_ref, ybuf, sem, *, tc, nt):
    i = pl.program_id(0)
    slot = lax.rem(i, 2)
    half = 2 * tc * ROW_PITCH

    def copy(dr, first, s, r, k):
        return pltpu.make_async_copy(y_hbm.at[dr[0, k, r]],
                                     ybuf.at[pl.ds(first + (k * tc + r) * ROW_PITCH, ROW_TILE)], sem.at[s])

    def gather(dr, first, s):
        def row(r, carry):
            copy(dr, first, s, r, 0).start()
            copy(dr, first, s, r, 1).start(priority=1)
            return carry
        lax.fori_loop(0, tc, row, 0, unroll=8)

    @pl.when(i == 0)
    def _():
        gather(d_ref, 0, 0)

    @pl.when(i + 1 < nt)
    def _():
        gather(d_next_ref, (1 - slot) * half, 1 - slot)

    for _ in range(2 * tc):
        copy(d_ref, slot * half, slot, 0, 0).wait()
    rw = rw_ref[...]
    y = (rw[:, 0:1] * _rows_from_tiles(ybuf, slot * half, tc, ROW_PITCH)
         + rw[:, 1:2] * _rows_from_tiles(ybuf, slot * half + tc * ROW_PITCH, tc, ROW_PITCH))
    o_ref[...] = _rms(h1_ref[...] + y, fn_ref[...])


def _combine(dst, y_rows, h2, rw2, final_norm, *, tc):
    n = h2.shape[0]
    nt = n // tc
    per = dst.shape[2] // tc
    blk = lambda i: (i // per, 0, lax.rem(i, per))
    return pl.pallas_call(
        functools.partial(_combine_kernel, tc=tc, nt=nt),
        grid=(nt,),
        in_specs=[pl.BlockSpec((1, 2, tc), lambda i: blk(i), memory_space=pltpu.SMEM),
                  pl.BlockSpec((1, 2, tc), lambda i: blk(jnp.minimum(i + 1, nt - 1)), memory_space=pltpu.SMEM),
                  pl.BlockSpec(memory_space=pl.ANY),
                  pl.BlockSpec((tc, D_MODEL), lambda i: (i, 0)),
                  pl.BlockSpec((tc, LANES), lambda i: (i, 0)),
                  pl.BlockSpec((1, D_MODEL), lambda i: (0, 0))],
        out_specs=pl.BlockSpec((tc, D_MODEL), lambda i: (i, 0)),
        out_shape=jax.ShapeDtypeStruct((n, D_MODEL), F32),
        scratch_shapes=[pltpu.VMEM((2 * 2 * tc * ROW_PITCH, LANES), F32), pltpu.SemaphoreType.DMA((2,))],
        compiler_params=pltpu.CompilerParams(dimension_semantics=("arbitrary",), vmem_limit_bytes=VMEM_LIMIT),
        name="combine",
    )(dst, dst, y_rows, h2, rw2, final_norm)


def _tile_plan(counts, n, *, tm):
    nt = (2 * n) // tm + N_EXPERTS
    tiles_e = (counts + tm - 1) // tm
    tile_end = jnp.cumsum(tiles_e)
    tile_start = tile_end - tiles_e
    k = jnp.arange(nt, dtype=jnp.int32)
    tile_expert = jnp.minimum(jnp.sum((k[:, None] >= tile_end[None, :]).astype(jnp.int32), axis=1), N_EXPERTS - 1)
    onehot = (tile_expert[:, None] == jnp.arange(N_EXPERTS)[None, :]).astype(jnp.int32)
    left = jnp.sum(onehot * (counts[None, :] - (k[:, None] - tile_start[None, :]) * tm), axis=1)
    tile_valid = jnp.where(k < tile_end[-1], jnp.clip(left, 0, tm), 0).astype(jnp.int32)
    return ((tile_start * tm).astype(jnp.int32), (tiles_e * tm).astype(jnp.int32),
            tile_end[-1:].astype(jnp.int32), tile_expert.astype(jnp.int32), tile_valid, nt)


def _pad_heads(w, width):
    k = w.shape[0]
    return jnp.pad(w.reshape(k, HEADS, width), ((0, 0), (0, 0), (0, HEAD_PAD - width))).reshape(k, HEADS * HEAD_PAD)


def _rope_tables(length):
    pos = np.arange(length, dtype=np.float64)
    inv = ROPE_THETA ** (-np.arange(0, MLA_ROPE, 2, dtype=np.float64) / MLA_ROPE)
    ang = pos[:, None] * inv[None, :]
    cos, sin = np.cos(ang), np.sin(ang)
    one = np.ones((length, ROPE_LO))
    zero_lo = np.zeros((length, ROPE_LO))
    zero_hi = np.zeros((length, LANES - ROPE_LO - MLA_ROPE))
    return (np.concatenate([one, cos, cos, zero_hi], axis=1).astype(np.float32),
            np.concatenate([zero_lo, -sin, sin, zero_hi], axis=1).astype(np.float32))


def _decay_columns():
    eq = np.zeros((LANES, HEADS * HEAD_PAD), np.float32)
    ek = np.zeros_like(eq)
    oq = np.zeros((1, HEADS * HEAD_PAD), np.float32)
    ok = np.zeros_like(oq)
    for h in range(HEADS):
        for p in range(N_SPLIT):
            eq[p * HEADS + h, h * HEAD_PAD + FOX_AUG + p] = 1.0
            ek[p * HEADS + h, h * HEAD_PAD + FOX_AUG + N_SPLIT + p] = -1.0
            oq[0, h * HEAD_PAD + FOX_AUG + N_SPLIT + p] = 1.0
            ok[0, h * HEAD_PAD + FOX_AUG + p] = 1.0
    return jnp.asarray(eq, BF16), jnp.asarray(ek, BF16), jnp.asarray(oq), jnp.asarray(ok)


def kernel(x, meta, attn_norm, w_in, b_forget, q_a_norm, w_q_up, kv_a_norm, w_kv_up, w_mla_out, w_fox_out, w_out, ffn_norm, w_group_router, b_group_router, w_expert_router, b_expert_router, w_gate, w_up, w_down, final_norm):
    b, seq, d = x.shape
    t_attn = 512
    row = lambda v: v.reshape(1, -1).astype(F32)

    w = w_in[0]
    offs = np.cumsum([0, Q_LORA, KV_LORA, MLA_ROPE, HEADS * FOX_HD, HEADS * FOX_HD, HEADS * FOX_HD,
                      HEADS, D_MODEL, D_MODEL])
    seg = lambda j: w[:, offs[j]:offs[j + 1]]
    zcol = lambda n: jnp.zeros((d, n), F32)
    w_a = jnp.concatenate(
        [seg(0), seg(1),
         zcol(ROPE_LO), seg(2), zcol(LANES - ROPE_LO - MLA_ROPE),
         seg(6), zcol(LANES - HEADS),
         seg(3), seg(4)], axis=1).astype(BF16)
    w_fv = seg(5).astype(BF16)
    w_gates = jnp.concatenate([seg(7), seg(8)], axis=1).astype(BF16)
    w_q = _pad_heads(w_q_up[0], MLA_NOPE + MLA_ROPE).astype(BF16)
    kv3 = w_kv_up[0].reshape(KV_LORA, HEADS, MLA_NOPE + MLA_V)
    w_k = kv3[:, :, :MLA_NOPE].reshape(KV_LORA, HEADS * MLA_NOPE).astype(BF16)
    w_v = kv3[:, :, MLA_NOPE:].reshape(KV_LORA, HEADS * MLA_V).astype(BF16)
    bf128 = jnp.pad(row(b_forget[0]), ((0, 0), (0, LANES - HEADS)))
    eq, ek, oq, ok = _decay_columns()
    cos, sin = _rope_tables(N_META + seq)
    consts = [row(attn_norm[0]), w_a, row(q_a_norm[0]), w_q, row(kv_a_norm[0]), w_k, w_v, w_fv, bf128,
              eq, ek, oq, ok]

    meta_pad = jnp.pad(meta.astype(F32), ((0, META_PAD - N_META), (0, 0)))[None]
    pad_tab = lambda tab: np.pad(tab[:N_META], ((0, META_PAD - N_META), (0, 0)))
    _, km_meta, vm_meta, _, kf_meta, vf_meta, c_meta = _projections(
        meta_pad, pad_tab(cos), pad_tab(sin), jnp.zeros((1, LANES), F32), consts, tm=META_PAD)
    c0 = c_meta[0, N_META - 1:N_META, :]

    qm, km, vm, qf, kf, vf, _ = _projections(x, cos[N_META:], sin[N_META:], c0, consts, tm=t_attn)
    ya_t = _attention(qm, km, vm, km_meta, vm_meta, t=ATTN_QUERIES, tk=t_attn, g=ATTN_HEADS_PER_STEP)
    yb_t = _attention(qf, kf, vf, kf_meta, vf_meta, t=ATTN_QUERIES, tk=t_attn, g=ATTN_HEADS_PER_STEP)

    w_r = jnp.concatenate([w_expert_router[0], w_group_router[0],
                           jnp.zeros((d, LANES - N_EXPERTS - N_GROUPS), F32)], axis=1)
    w_rh = w_r.astype(BF16)
    w_rhl = jnp.concatenate([w_rh, (w_r - w_rh.astype(F32)).astype(BF16)], axis=1)
    b_r = jnp.pad(jnp.concatenate([row(b_expert_router[0]), row(b_group_router[0])], axis=1),
                  ((0, 0), (0, LANES - N_EXPERTS - N_GROUPS)))
    h1, t_rows, ri, rw, cnt = _merge(
        x, ya_t, yb_t,
        [row(attn_norm[0]), w_gates, w_mla_out[0].astype(BF16), w_fox_out[0].astype(BF16),
         w_out[0].astype(BF16), row(ffn_norm[0]), w_rhl, b_r], tm=512)

    n = b * seq
    counts = cnt[0, :N_EXPERTS].astype(jnp.int32)
    row_start, padded, tiles_used, tile_expert, tile_valid, nt = _tile_plan(counts, n, tm=EXPERT_ROWS)
    dst = _dst_rows(row_start, ri)
    xs = _dispatch(row_start, counts, padded, tiles_used, dst, t_rows, nt, tm=512, te=EXPERT_ROWS)
    y_rows = _experts(tile_expert, tile_valid, xs,
                      w_gate[0].reshape(N_EXPERTS, d, D_EXPERT), w_up[0].reshape(N_EXPERTS, d, D_EXPERT),
                      w_down[0].reshape(N_EXPERTS, D_EXPERT, d), tm=EXPERT_ROWS)
    out = _combine(dst, y_rows.reshape(-1, ROW_TILE, LANES), h1.reshape(n, d), rw.reshape(n, LANES),
                   row(final_norm), tc=COMBINE_TOKENS)
    return out.reshape(b, seq, d)
```

```python
import functools

import numpy as np
import jax
import jax.numpy as jnp
from jax import lax
from jax.experimental import pallas as pl
from jax.experimental.pallas import tpu as pltpu

F32 = jnp.float32
BF16 = jnp.bfloat16

D_MODEL = 1024
N_META = 16
HEADS = 8
MLA_NOPE, MLA_ROPE, MLA_V = 64, 32, 64
Q_LORA, KV_LORA = 512, 256
ROPE_THETA = 10000.0
FOX_HD = 64
N_GROUPS, EXPERTS_PER_GROUP, D_EXPERT = 4, 8, 256
N_EXPERTS = N_GROUPS * EXPERTS_PER_GROUP
EPS = 1e-6
NEG = -1e30
LOG2E = 1.4426950408889634

LANES = 128
HEAD_PAD = LANES
META_PAD = LANES
ROPE_LO = MLA_NOPE
ROPE_HALF = MLA_ROPE // 2
FOX_AUG = FOX_HD
N_SPLIT = 3

VMEM_LIMIT = 56 * 1024 * 1024
ATTN_QUERIES = 1024
ATTN_HEADS_PER_STEP = 8
SKEW = 2
ACC_PAD = 16
ROW_TILE = D_MODEL // LANES
ROW_PITCH = ROW_TILE + 1
IDX_FIELDS = 4
X_AHEAD = 2
EXPERT_ROWS = 256
COMBINE_TOKENS = 256

_NT = (((1,), (1,)), ((), ()))
_TN = (((0,), (0,)), ((), ()))
_TT = (((0,), (1,)), ((), ()))


def _rms(x, g):
    return x * lax.rsqrt(jnp.mean(x * x, axis=-1, keepdims=True) + EPS) * g


def _rows_to_tiles(ref2d, v, pitch=ROW_TILE):
    n = v.shape[0]
    for c in range(ROW_TILE):
        ref2d[pl.ds(c, n, stride=pitch), :] = v[:, c * LANES:(c + 1) * LANES]
    for c in range(ROW_TILE, pitch):
        ref2d[pl.ds(c, n, stride=pitch), :] = jnp.zeros((n, LANES), v.dtype)


def _rows_from_tiles(ref2d, first, n, pitch):
    return jnp.concatenate([ref2d[pl.ds(first + c, n, stride=pitch), :] for c in range(ROW_TILE)], axis=1)


def _split3(v):
    hi = v.astype(BF16)
    r = v - hi.astype(F32)
    mid = r.astype(BF16)
    lo = (r - mid.astype(F32)).astype(BF16)
    return hi, mid, lo


def _rope_tile(v, c, s):
    lane = lax.broadcasted_iota(jnp.int32, v.shape, 1)
    partner = jnp.where(lane < ROPE_LO + ROPE_HALF,
                        pltpu.roll(v, LANES - ROPE_HALF, 1), pltpu.roll(v, ROPE_HALF, 1))
    return v * c + partner * s


def _proj_kernel(x_ref, cos_ref, sin_ref, c0_ref, an_ref, wa_ref, qn_ref, wq_ref, kvn_ref, wk_ref,
                 wv_ref, wfv_ref, bf_ref, eq_ref, ek_ref, oq_ref, ok_ref,
                 qm_ref, km_ref, vm_ref, qf_ref, kf_ref, vf_ref, cl_ref, carry_ref, *, tm):
    i = pl.program_id(1)

    @pl.when(i == 0)
    def _():
        carry_ref[...] = jnp.broadcast_to(c0_ref[...], carry_ref.shape)

    ub = _rms(x_ref[0], an_ref[...]).astype(BF16)
    pa = jnp.dot(ub, wa_ref[...], preferred_element_type=F32)
    c_q = pa[:, 0:Q_LORA]
    c_kv = pa[:, Q_LORA:Q_LORA + KV_LORA]
    o = Q_LORA + KV_LORA
    kr = pa[:, o:o + LANES]
    fl = pa[:, o + LANES:o + 2 * LANES]
    o += 2 * LANES
    fq = pa[:, o:o + HEADS * FOX_HD]
    fk = pa[:, o + HEADS * FOX_HD:o + 2 * HEADS * FOX_HD]

    cos = cos_ref[...]
    sin = sin_ref[...]

    cqn = _rms(c_q, qn_ref[...]).astype(BF16)
    q = jnp.dot(cqn, wq_ref[...], preferred_element_type=F32)
    ckvn = _rms(c_kv, kvn_ref[...]).astype(BF16)
    kb = jnp.dot(ckvn, wk_ref[...], preferred_element_type=F32)
    krr = _rope_tile(kr, cos, sin)
    q_scale = (MLA_NOPE + MLA_ROPE) ** -0.5 * LOG2E
    lane = lax.broadcasted_iota(jnp.int32, krr.shape, 1)
    for h in range(HEADS):
        sl = slice(h * HEAD_PAD, (h + 1) * HEAD_PAD)
        qm_ref[0, h] = (_rope_tile(q[:, sl], cos, sin) * q_scale).astype(BF16)
        kh = kb[:, (h // 2) * LANES:(h // 2 + 1) * LANES]
        if h % 2:
            kh = pltpu.roll(kh, LANES - MLA_NOPE, 1)
        km_ref[0, h] = jnp.where(lane < MLA_NOPE, kh, krr).astype(BF16)
    vt = lax.dot_general(wv_ref[...], ckvn, _TT, preferred_element_type=F32)
    for h in range(HEADS):
        vm_ref[0, h, 0] = vt[h * MLA_V:(h + 1) * MLA_V, :].astype(BF16)

    z = fl + bf_ref[...]
    logf = jnp.minimum(z, 0.0) - jnp.log1p(jnp.exp(-jnp.abs(z)))
    r_i = lax.broadcasted_iota(jnp.int32, (tm, tm), 0)
    c_i = lax.broadcasted_iota(jnp.int32, (tm, tm), 1)
    tri = jnp.where(c_i <= r_i, 1.0, 0.0).astype(BF16)
    hi, mid, lo = _split3(logf)
    cs2 = jnp.dot(tri, jnp.concatenate([hi, mid], axis=1), preferred_element_type=F32)
    cs = cs2[:, :LANES] + cs2[:, LANES:] + jnp.dot(tri, lo, preferred_element_type=F32)
    c = carry_ref[0:1, :] + cs
    carry_ref[...] = jnp.broadcast_to(c[tm - 1:tm, :], carry_ref.shape)
    cl_ref[0] = c
    p_hi, p_mid, p_lo = (piece.astype(F32) for piece in _split3(c * LOG2E))
    cat = jnp.where(lane < HEADS, p_hi,
                    jnp.where(lane < 2 * HEADS, pltpu.roll(p_mid, HEADS, 1),
                              jnp.where(lane < 3 * HEADS, pltpu.roll(p_lo, 2 * HEADS, 1), 0.0))).astype(BF16)
    augq = jnp.dot(cat, eq_ref[...], preferred_element_type=F32) + oq_ref[...]
    augk = jnp.dot(cat, ek_ref[...], preferred_element_type=F32) + ok_ref[...]
    fqs = fq * (FOX_HD ** -0.5 * LOG2E)
    for h in range(HEADS):
        pair = slice((h // 2) * LANES, (h // 2 + 1) * LANES)
        sl = slice(h * HEAD_PAD, (h + 1) * HEAD_PAD)
        qh, kh = fqs[:, pair], fk[:, pair]
        if h % 2:
            qh, kh = pltpu.roll(qh, LANES - FOX_HD, 1), pltpu.roll(kh, LANES - FOX_HD, 1)
        qf_ref[0, h] = jnp.where(lane < FOX_HD, qh, augq[:, sl]).astype(BF16)
        kf_ref[0, h] = jnp.where(lane < FOX_HD, kh, augk[:, sl]).astype(BF16)
    vft = lax.dot_general(wfv_ref[...], ub, _TT, preferred_element_type=F32)
    for h in range(HEADS):
        vf_ref[0, h, 0] = vft[h * FOX_HD:(h + 1) * FOX_HD, :].astype(BF16)


def _projections(x3, cos, sin, c0, consts, *, tm):
    b, l, _ = x3.shape
    nt = l // tm
    full = lambda a: pl.BlockSpec(a.shape, lambda bi, i: (0,) * a.ndim)
    qk_shape = jax.ShapeDtypeStruct((b, HEADS, l, HEAD_PAD), BF16)
    vt_shape = jax.ShapeDtypeStruct((b, HEADS, nt, MLA_V, tm), BF16)
    qk_spec = pl.BlockSpec((1, HEADS, tm, HEAD_PAD), lambda bi, i: (bi, 0, i, 0))
    vt_spec = pl.BlockSpec((1, HEADS, 1, MLA_V, tm), lambda bi, i: (bi, 0, i, 0, 0))
    return pl.pallas_call(
        functools.partial(_proj_kernel, tm=tm),
        grid=(b, nt),
        in_specs=[pl.BlockSpec((1, tm, D_MODEL), lambda bi, i: (bi, i, 0)),
                  pl.BlockSpec((tm, LANES), lambda bi, i: (i, 0)),
                  pl.BlockSpec((tm, LANES), lambda bi, i: (i, 0)),
                  full(c0)] + [full(a) for a in consts],
        out_specs=[qk_spec, qk_spec, vt_spec, qk_spec, qk_spec, vt_spec,
                   pl.BlockSpec((1, tm, LANES), lambda bi, i: (bi, i, 0))],
        out_shape=[qk_shape, qk_shape, vt_shape, qk_shape, qk_shape, vt_shape,
                   jax.ShapeDtypeStruct((b, l, LANES), F32)],
        scratch_shapes=[pltpu.VMEM((8, LANES), F32)],
        compiler_params=pltpu.CompilerParams(
            dimension_semantics=("arbitrary", "arbitrary"), vmem_limit_bytes=VMEM_LIMIT),
        name="proj",
    )(x3, cos, sin, c0, *consts)


def _attn_kernel(q_ref, k_ref, vt_ref, km_ref, vtm_ref, o_ref, m_ref, acc_ref, *, t, tk, g):
    i = pl.program_id(2)
    dv = vt_ref.shape[3]

    def ones_rows(n):
        return jnp.where(lax.broadcasted_iota(jnp.int32, (ACC_PAD, n), 0) == 0, 1.0, 0.0).astype(BF16)

    def update(hd, s, vt_blk, lo):
        m_prev = m_ref[hd, :, lo:]
        m_new = jnp.maximum(m_prev, jnp.max(s, axis=0, keepdims=True))
        alpha = jnp.exp2(m_prev - m_new)
        p = jnp.exp2(s - m_new).astype(BF16)
        vt_aug = jnp.concatenate([vt_blk, ones_rows(p.shape[0])], axis=0)
        pv = jnp.dot(vt_aug, p, preferred_element_type=F32)
        acc_ref[hd, :, lo:] = alpha * acc_ref[hd, :, lo:] + pv
        m_ref[hd, :, lo:] = m_new

    def run(items):
        def scores(n):
            hd, k_of, _, _, lo = items[n]
            return lax.dot_general(k_of(), q_ref[0, hd, lo:, :], _NT, preferred_element_type=F32)
        pending = [scores(n) for n in range(SKEW)]
        for n, (hd, _, vt_of, mask, lo) in enumerate(items):
            if n + SKEW < len(items):
                pending.append(scores(n + SKEW))
            s = pending.pop(0)
            if mask is not None:
                s = jnp.where(mask, s, NEG)
            update(hd, s, vt_of(), lo)

    def block_items(j, mask=None, lo=0):
        off = pl.multiple_of(j * tk, tk)
        return [(hd, lambda hd=hd: k_ref[0, hd, pl.ds(off, tk), :], lambda hd=hd: vt_ref[0, hd, j], mask, lo)
                for hd in range(g)]

    m_ref[...] = jnp.full(m_ref.shape, NEG, F32)
    acc_ref[...] = jnp.zeros(acc_ref.shape, F32)

    per = t // tk

    def body(jj, carry):
        run([it for d in range(per) for it in block_items(jj * per + d)])
        return carry

    lax.fori_loop(0, i, body, 0)

    tail = []
    for d in range(per):
        key_row = lax.broadcasted_iota(jnp.int32, (tk, t - d * tk), 0)
        query = lax.broadcasted_iota(jnp.int32, (tk, t - d * tk), 1)
        tail += block_items(i * per + d, key_row <= query, d * tk)
    row = lax.broadcasted_iota(jnp.int32, (META_PAD, t), 0)
    tail += [(hd, lambda hd=hd: km_ref[0, hd], lambda hd=hd: vtm_ref[0, hd, 0], row < N_META, 0)
             for hd in range(g)]
    run(tail)

    for hd in range(g):
        a = acc_ref[hd]
        o_ref[0, hd * dv:(hd + 1) * dv, :] = (a[:dv] / a[dv:dv + 1]).astype(o_ref.dtype)


def _attention(q, k, vt, k_meta, vt_meta, *, t, tk, g):
    b, h, l, _ = q.shape
    nq = l // t
    dv = vt.shape[3]
    return pl.pallas_call(
        functools.partial(_attn_kernel, t=t, tk=tk, g=g),
        grid=(b, h // g, nq),
        in_specs=[pl.BlockSpec((1, g, t, HEAD_PAD), lambda bi, hi, i: (bi, hi, i, 0)),
                  pl.BlockSpec((1, g, l, HEAD_PAD), lambda bi, hi, i: (bi, hi, 0, 0),
                               pipeline_mode=pl.Buffered(1)),
                  pl.BlockSpec((1, g, l // tk, dv, tk), lambda bi, hi, i: (bi, hi, 0, 0, 0),
                               pipeline_mode=pl.Buffered(1)),
                  pl.BlockSpec((1, g, META_PAD, HEAD_PAD), lambda bi, hi, i: (0, hi, 0, 0)),
                  pl.BlockSpec((1, g, 1, dv, META_PAD), lambda bi, hi, i: (0, hi, 0, 0, 0))],
        out_specs=pl.BlockSpec((1, g * dv, t), lambda bi, hi, i: (bi, hi, i)),
        out_shape=jax.ShapeDtypeStruct((b, h * dv, l), BF16),
        scratch_shapes=[pltpu.VMEM((g, 1, t), F32), pltpu.VMEM((g, dv + ACC_PAD, t), F32)],
        compiler_params=pltpu.CompilerParams(
            dimension_semantics=("arbitrary", "arbitrary", "arbitrary"), vmem_limit_bytes=VMEM_LIMIT),
        name="attn",
    )(q, k, vt, k_meta, vt_meta)


def _merge_kernel(x_ref, ya_ref, yb_ref, an_ref, wg_ref, wmo_ref, wfo_ref, wo_ref, fn_ref,
                  wrhl_ref, br_ref, h1_ref, t_ref, ri_ref, rw_ref, cnt_ref):
    @pl.when((pl.program_id(0) == 0) & (pl.program_id(1) == 0))
    def _():
        cnt_ref[...] = jnp.zeros_like(cnt_ref)

    x = x_ref[0]
    ub = _rms(x, an_ref[...]).astype(BF16)
    g = jnp.dot(ub, wg_ref[...], preferred_element_type=F32)
    ya = lax.dot_general(ya_ref[0], wmo_ref[...], _TN, preferred_element_type=F32)
    yb = lax.dot_general(yb_ref[0], wfo_ref[...], _TN, preferred_element_type=F32)
    merged = jax.nn.sigmoid(g[:, :D_MODEL]) * ya + jax.nn.sigmoid(g[:, D_MODEL:]) * yb
    h1 = x + jnp.dot(merged.astype(BF16), wo_ref[...], preferred_element_type=F32)
    h1_ref[0] = h1
    t = _rms(h1, fn_ref[...])
    _rows_to_tiles(t_ref, t, ROW_PITCH)

    th = t.astype(BF16)
    tl = (t - th.astype(F32)).astype(BF16)
    hl = jnp.dot(th, wrhl_ref[...], preferred_element_type=F32)
    logits = (hl[:, :LANES] + hl[:, LANES:]
              + jnp.dot(tl, wrhl_ref[:, :LANES], preferred_element_type=F32)) + br_ref[...]
    tm = logits.shape[0]
    lt = logits.T
    row = lax.broadcasted_iota(jnp.int32, lt.shape, 0)
    big = jnp.int32(LANES)

    def first_argmax(v, vmax):
        return jnp.min(jnp.where(v == vmax, row, big), axis=0, keepdims=True)

    gmask = (row >= N_EXPERTS) & (row < N_EXPERTS + N_GROUPS)
    gl = jnp.where(gmask, lt, NEG)
    gmax = jnp.max(gl, axis=0, keepdims=True)
    g_w = 1.0 / jnp.sum(jnp.where(gmask, jnp.exp(gl - gmax), 0.0), axis=0, keepdims=True)
    g_idx = first_argmax(gl, gmax) - N_EXPERTS
    emask = (row < N_EXPERTS) & (jnp.right_shift(row, 3) == g_idx)
    el = jnp.where(emask, lt, NEG)
    m1 = jnp.max(el, axis=0, keepdims=True)
    i1 = first_argmax(el, m1)
    el2 = jnp.where(row == i1, NEG, el)
    m2 = jnp.max(el2, axis=0, keepdims=True)
    i2 = first_argmax(el2, m2)
    zsum = jnp.sum(jnp.where(emask, jnp.exp(el - m1), 0.0), axis=0, keepdims=True)
    p1 = 1.0 / zsum
    p2 = jnp.exp(m2 - m1) / zsum
    tot = p1 + p2
    w1 = g_w * (p1 / tot)
    w2 = g_w * (p2 / tot)
    rw_ref[0] = jnp.where(row == 0, w1, jnp.where(row == 1, w2, 0.0)).T

    hot1 = row == i1
    hot2 = row == i2
    onehot = jnp.where(hot1, 1.0, jnp.where(hot2, 1.0, 0.0)).astype(BF16)
    c_i = lax.broadcasted_iota(jnp.int32, (tm, tm), 0)
    r_i = lax.broadcasted_iota(jnp.int32, (tm, tm), 1)
    earlier = jnp.where(c_i < r_i, 1.0, 0.0).astype(BF16)
    cnt = cnt_ref[...]
    rank = (jnp.concatenate([cnt] * (tm // LANES), axis=1)
            + jnp.dot(onehot, earlier, preferred_element_type=F32))
    rank1 = jnp.sum(jnp.where(hot1, rank, 0.0), axis=0, keepdims=True).astype(jnp.int32)
    rank2 = jnp.sum(jnp.where(hot2, rank, 0.0), axis=0, keepdims=True).astype(jnp.int32)
    field = lax.broadcasted_iota(jnp.int32, (8, tm), 0)
    fields = jnp.where(field == 0, i1, jnp.where(field == 1, i2,
                       jnp.where(field == 2, rank1, jnp.where(field == 3, rank2, 0))))
    ri_ref[0] = fields[0:IDX_FIELDS, :]
    cnt_ref[...] = cnt + jnp.dot(onehot, jnp.ones((tm, LANES), BF16), preferred_element_type=F32)


def _merge(x3, ya_t, yb_t, consts, *, tm):
    b, l, _ = x3.shape
    full = lambda a: pl.BlockSpec(a.shape, lambda bi, i: (0,) * a.ndim)
    tok = lambda w: pl.BlockSpec((1, tm, w), lambda bi, i: (bi, i, 0))
    yspec = pl.BlockSpec((1, ya_t.shape[1], tm), lambda bi, i: (bi, 0, i))
    return pl.pallas_call(
        _merge_kernel,
        grid=(b, l // tm),
        in_specs=[tok(D_MODEL), yspec, yspec] + [full(a) for a in consts],
        out_specs=[tok(D_MODEL),
                   pl.BlockSpec((tm * ROW_PITCH, LANES), lambda bi, i: (bi * (l // tm) + i, 0)),
                   pl.BlockSpec((1, IDX_FIELDS, tm), lambda bi, i: (bi * (l // tm) + i, 0, 0)), tok(LANES),
                   pl.BlockSpec((LANES, LANES), lambda bi, i: (0, 0))],
        out_shape=[jax.ShapeDtypeStruct((b, l, D_MODEL), F32),
                   jax.ShapeDtypeStruct((b * l * ROW_PITCH, LANES), F32),
                   jax.ShapeDtypeStruct((b * (l // tm), IDX_FIELDS, tm), jnp.int32),
                   jax.ShapeDtypeStruct((b, l, LANES), F32),
                   jax.ShapeDtypeStruct((LANES, LANES), F32)],
        compiler_params=pltpu.CompilerParams(
            dimension_semantics=("arbitrary", "arbitrary"), vmem_limit_bytes=VMEM_LIMIT),
        name="merge",
    )(x3, ya_t, yb_t, *consts)


def _dst_rows_kernel(rs_ref, idx_ref, d_ref):
    for k in range(2):
        e = idx_ref[:, k, :]
        start = jnp.zeros_like(e)
        for j in range(N_EXPERTS):
            start = jnp.where(e == j, rs_ref[j], start)
        d_ref[:, k, :] = start + idx_ref[:, 2 + k, :]


def _dst_rows(row_start, idx):
    nt, _, tm = idx.shape
    return pl.pallas_call(
        _dst_rows_kernel,
        grid_spec=pltpu.PrefetchScalarGridSpec(
            num_scalar_prefetch=1, grid=(1,),
            in_specs=[pl.BlockSpec(idx.shape, lambda i, rs: (0, 0, 0))],
            out_specs=pl.BlockSpec((nt, 2, tm), lambda i, rs: (0, 0, 0))),
        out_shape=jax.ShapeDtypeStruct((nt, 2, tm), jnp.int32),
        name="dst_rows",
    )(row_start, idx)


def _dispatch_kernel(rs_ref, cnt_ref, pad_ref, used_ref, d_ref, t_ref, xs_hbm, zero_ref, sem, zsem, *,
                     tm, te, nt):
    def row_tile(row):
        return xs_hbm.at[pl.ds(row * ROW_PITCH, ROW_PITCH)]

    def zero_rows(first, n):
        return pltpu.make_async_copy(zero_ref.at[pl.ds(0, n * ROW_PITCH)],
                                     xs_hbm.at[pl.ds(first * ROW_PITCH, n * ROW_PITCH)], zsem)

    def pad_copies(act):
        for e in range(N_EXPERTS):
            n_pad = pad_ref[e] - cnt_ref[e]
            first = rs_ref[e] + cnt_ref[e]
            size = te // 2
            while size:
                @pl.when((n_pad & size) != 0)
                def _(size=size, n_pad=n_pad, first=first):
                    act(zero_rows(first + (n_pad & ~(2 * size - 1)), size))
                size //= 2

    def tail_copies(act):
        def tile(k, carry):
            act(zero_rows(k * te, te))
            return carry
        lax.fori_loop(used_ref[0], nt, tile, 0)

    @pl.when(pl.program_id(0) == 0)
    def _():
        zero_ref[...] = jnp.zeros_like(zero_ref)
        pad_copies(lambda cp: cp.start())
        tail_copies(lambda cp: cp.start())
        pad_copies(lambda cp: cp.wait())
        tail_copies(lambda cp: cp.wait())

    def copy(r, k):
        return pltpu.make_async_copy(t_ref.at[pl.ds(r * ROW_PITCH, ROW_PITCH)], row_tile(d_ref[0, k, r]), sem)

    def row(r, carry):
        copy(r, 0).start()
        copy(r, 1).start(priority=1)
        return carry
    lax.fori_loop(0, tm, row, 0, unroll=8)
    for _ in range(2 * tm):
        copy(0, 0).wait()


def _dispatch(row_start, counts, padded, tiles_used, dst, t_rows, n_tiles, *, tm, te):
    grid_spec = pltpu.PrefetchScalarGridSpec(
        num_scalar_prefetch=4,
        grid=(dst.shape[0],),
        in_specs=[pl.BlockSpec((1, 2, tm), lambda i, *_: (i, 0, 0), memory_space=pltpu.SMEM),
                  pl.BlockSpec((tm * ROW_PITCH, LANES), lambda i, *_: (i, 0))],
        out_specs=pl.BlockSpec(memory_space=pl.ANY),
        scratch_shapes=[pltpu.VMEM((te * ROW_PITCH, LANES), F32), pltpu.SemaphoreType.DMA(()),
                        pltpu.SemaphoreType.DMA(())])
    return pl.pallas_call(
        functools.partial(_dispatch_kernel, tm=tm, te=te, nt=n_tiles),
        grid_spec=grid_spec,
        out_shape=jax.ShapeDtypeStruct((n_tiles * te * ROW_PITCH, LANES), F32),
        compiler_params=pltpu.CompilerParams(dimension_semantics=("arbitrary",), vmem_limit_bytes=VMEM_LIMIT),
        name="dispatch",
    )(row_start, counts, padded, tiles_used, dst, t_rows)


def _expert_kernel(te_ref, tv_ref, xs_hbm, wg_ref, wu_ref, wd_ref, y_ref, xbuf, sem, *, tm, nt):
    del te_ref
    i = pl.program_id(0)
    rows = tm * ROW_PITCH

    def fetch(k):
        slot = lax.rem(k, X_AHEAD + 1)
        return pltpu.make_async_copy(xs_hbm.at[pl.ds(k * rows, rows)], xbuf.at[slot], sem.at[slot])

    @pl.when(i == 0)
    def _():
        for k in range(X_AHEAD):
            @pl.when(tv_ref[k] > 0)
            def _():
                fetch(k).start()

    ahead = jnp.minimum(i + X_AHEAD, nt - 1)

    @pl.when((i + X_AHEAD < nt) & (tv_ref[ahead] > 0))
    def _():
        fetch(ahead).start()

    @pl.when(tv_ref[i] > 0)
    def _():
        fetch(i).wait()
        x = _rows_from_tiles(xbuf.at[lax.rem(i, X_AHEAD + 1)], 0, tm, ROW_PITCH).astype(BF16)
        a = jnp.dot(x, wg_ref[0].astype(BF16), preferred_element_type=F32)
        u = jnp.dot(x, wu_ref[0].astype(BF16), preferred_element_type=F32)
        hm = (a * jax.nn.sigmoid(a)) * u
        _rows_to_tiles(y_ref, jnp.dot(hm.astype(BF16), wd_ref[0].astype(BF16), preferred_element_type=F32))

    @pl.when(tv_ref[i] == 0)
    def _():
        y_ref[...] = jnp.zeros_like(y_ref)


def _experts(tile_expert, tile_valid, xs, w_gate, w_up, w_down, *, tm):
    nt = tile_expert.shape[0]
    wspec = lambda shape: pl.BlockSpec((1,) + shape, lambda i, te, tv: (te[i], 0, 0))
    grid_spec = pltpu.PrefetchScalarGridSpec(
        num_scalar_prefetch=2,
        grid=(nt,),
        in_specs=[pl.BlockSpec(memory_space=pl.ANY),
                  wspec((D_MODEL, D_EXPERT)), wspec((D_MODEL, D_EXPERT)), wspec((D_EXPERT, D_MODEL))],
        out_specs=pl.BlockSpec((tm * ROW_TILE, LANES), lambda i, te, tv: (i, 0)),
        scratch_shapes=[pltpu.VMEM((X_AHEAD + 1, tm * ROW_PITCH, LANES), F32),
                        pltpu.SemaphoreType.DMA((X_AHEAD + 1,))])
    return pl.pallas_call(
        functools.partial(_expert_kernel, tm=tm, nt=nt),
        grid_spec=grid_spec,
        out_shape=jax.ShapeDtypeStruct((nt * tm * ROW_TILE, LANES), F32),
        compiler_params=pltpu.CompilerParams(dimension_semantics=("arbitrary",), vmem_limit_bytes=VMEM_LIMIT),
        name="experts",
    )(tile_expert, tile_valid, xs, w_gate, w_up, w_down)


def _combine_kernel(d_ref, d_next_ref, y_hbm, h1_ref, rw_ref, fn_ref, o_ref, ybuf, sem, *, tc, nt):
    i = pl.program_id(0)
    slot = lax.rem(i, 2)
    half = 2 * tc * ROW_PITCH

    def copy(dr, first, s, r, k):
        return pltpu.make_async_copy(y_hbm.at[dr[0, k, r]],
                                     ybuf.at[pl.ds(first + (k * tc + r) * ROW_PITCH, ROW_TILE)], sem.at[s])

    def gather(dr, first, s):
        def row(r, carry):
            copy(dr, first, s, r, 0).start()
            copy(dr, first, s, r, 1).start(priority=1)
            return carry
        lax.fori_loop(0, tc, row, 0, unroll=8)

    @pl.when(i == 0)
    def _():
        gather(d_ref, 0, 0)

    @pl.when(i + 1 < nt)
    def _():
        gather(d_next_ref, (1 - slot) * half, 1 - slot)

    for _ in range(2 * tc):
        copy(d_ref, slot * half, slot, 0, 0).wait()
    rw = rw_ref[...]
    y = (rw[:, 0:1] * _rows_from_tiles(ybuf, slot * half, tc, ROW_PITCH)
         + rw[:, 1:2] * _rows_from_tiles(ybuf, slot * half + tc * ROW_PITCH, tc, ROW_PITCH))
    o_ref[...] = _rms(h1_ref[...] + y, fn_ref[...])


def _combine(dst, y_rows, h2, rw2, final_norm, *, tc):
    n = h2.shape[0]
    nt = n // tc
    per = dst.shape[2] // tc
    blk = lambda i: (i // per, 0, lax.rem(i, per))
    return pl.pallas_call(
        functools.partial(_combine_kernel, tc=tc, nt=nt),
        grid=(nt,),
        in_specs=[pl.BlockSpec((1, 2, tc), lambda i: blk(i), memory_space=pltpu.SMEM),
                  pl.BlockSpec((1, 2, tc), lambda i: blk(jnp.minimum(i + 1, nt - 1)), memory_space=pltpu.SMEM),
                  pl.BlockSpec(memory_space=pl.ANY),
                  pl.BlockSpec((tc, D_MODEL), lambda i: (i, 0)),
                  pl.BlockSpec((tc, LANES), lambda i: (i, 0)),
                  pl.BlockSpec((1, D_MODEL), lambda i: (0, 0))],
        out_specs=pl.BlockSpec((tc, D_MODEL), lambda i: (i, 0)),
        out_shape=jax.ShapeDtypeStruct((n, D_MODEL), F32),
        scratch_shapes=[pltpu.VMEM((2 * 2 * tc * ROW_PITCH, LANES), F32), pltpu.SemaphoreType.DMA((2,))],
        compiler_params=pltpu.CompilerParams(dimension_semantics=("arbitrary",), vmem_limit_bytes=VMEM_LIMIT),
        name="combine",
    )(dst, dst, y_rows, h2, rw2, final_norm)


def _tile_plan(counts, n, *, tm):
    nt = (2 * n) // tm + N_EXPERTS
    tiles_e = (counts + tm - 1) // tm
    tile_end = jnp.cumsum(tiles_e)
    tile_start = tile_end - tiles_e
    k = jnp.arange(nt, dtype=jnp.int32)
    tile_expert = jnp.minimum(jnp.sum((k[:, None] >= tile_end[None, :]).astype(jnp.int32), axis=1), N_EXPERTS - 1)
    onehot = (tile_expert[:, None] == jnp.arange(N_EXPERTS)[None, :]).astype(jnp.int32)
    left = jnp.sum(onehot * (counts[None, :] - (k[:, None] - tile_start[None, :]) * tm), axis=1)
    tile_valid = jnp.where(k < tile_end[-1], jnp.clip(left, 0, tm), 0).astype(jnp.int32)
    return ((tile_start * tm).astype(jnp.int32), (tiles_e * tm).astype(jnp.int32),
            tile_end[-1:].astype(jnp.int32), tile_expert.astype(jnp.int32), tile_valid, nt)


def _pad_heads(w, width):
    k = w.shape[0]
    return jnp.pad(w.reshape(k, HEADS, width), ((0, 0), (0, 0), (0, HEAD_PAD - width))).reshape(k, HEADS * HEAD_PAD)


def _rope_tables(length):
    pos = np.arange(length, dtype=np.float64)
    inv = ROPE_THETA ** (-np.arange(0, MLA_ROPE, 2, dtype=np.float64) / MLA_ROPE)
    ang = pos[:, None] * inv[None, :]
    cos, sin = np.cos(ang), np.sin(ang)
    one = np.ones((length, ROPE_LO))
    zero_lo = np.zeros((length, ROPE_LO))
    zero_hi = np.zeros((length, LANES - ROPE_LO - MLA_ROPE))
    return (np.concatenate([one, cos, cos, zero_hi], axis=1).astype(np.float32),
            np.concatenate([zero_lo, -sin, sin, zero_hi], axis=1).astype(np.float32))


def _decay_columns():
    eq = np.zeros((LANES, HEADS * HEAD_PAD), np.float32)
    ek = np.zeros_like(eq)
    oq = np.zeros((1, HEADS * HEAD_PAD), np.float32)
    ok = np.zeros_like(oq)
    for h in range(HEADS):
        for p in range(N_SPLIT):
            eq[p * HEADS + h, h * HEAD_PAD + FOX_AUG + p] = 1.0
            ek[p * HEADS + h, h * HEAD_PAD + FOX_AUG + N_SPLIT + p] = -1.0
            oq[0, h * HEAD_PAD + FOX_AUG + N_SPLIT + p] = 1.0
            ok[0, h * HEAD_PAD + FOX_AUG + p] = 1.0
    return jnp.asarray(eq, BF16), jnp.asarray(ek, BF16), jnp.asarray(oq), jnp.asarray(ok)


def kernel(x, meta, attn_norm, w_in, b_forget, q_a_norm, w_q_up, kv_a_norm, w_kv_up, w_mla_out, w_fox_out, w_out, ffn_norm, w_group_router, b_group_router, w_expert_router, b_expert_router, w_gate, w_up, w_down, final_norm):
    b, seq, d = x.shape
    t_attn = 512
    row = lambda v: v.reshape(1, -1).astype(F32)

    w = w_in[0]
    offs = np.cumsum([0, Q_LORA, KV_LORA, MLA_ROPE, HEADS * FOX_HD, HEADS * FOX_HD, HEADS * FOX_HD,
                      HEADS, D_MODEL, D_MODEL])
    seg = lambda j: w[:, offs[j]:offs[j + 1]]
    zcol = lambda n: jnp.zeros((d, n), F32)
    w_a = jnp.concatenate(
        [seg(0), seg(1),
         zcol(ROPE_LO), seg(2), zcol(LANES - ROPE_LO - MLA_ROPE),
         seg(6), zcol(LANES - HEADS),
         seg(3), seg(4)], axis=1).astype(BF16)
    w_fv = seg(5).astype(BF16)
    w_gates = jnp.concatenate([seg(7), seg(8)], axis=1).astype(BF16)
    w_q = _pad_heads(w_q_up[0], MLA_NOPE + MLA_ROPE).astype(BF16)
    kv3 = w_kv_up[0].reshape(KV_LORA, HEADS, MLA_NOPE + MLA_V)
    w_k = kv3[:, :, :MLA_NOPE].reshape(KV_LORA, HEADS * MLA_NOPE).astype(BF16)
    w_v = kv3[:, :, MLA_NOPE:].reshape(KV_LORA, HEADS * MLA_V).astype(BF16)
    bf128 = jnp.pad(row(b_forget[0]), ((0, 0), (0, LANES - HEADS)))
    eq, ek, oq, ok = _decay_columns()
    cos, sin = _rope_tables(N_META + seq)
    consts = [row(attn_norm[0]), w_a, row(q_a_norm[0]), w_q, row(kv_a_norm[0]), w_k, w_v, w_fv, bf128,
              eq, ek, oq, ok]

    meta_pad = jnp.pad(meta.astype(F32), ((0, META_PAD - N_META), (0, 0)))[None]
    pad_tab = lambda tab: np.pad(tab[:N_META], ((0, META_PAD - N_META), (0, 0)))
    _, km_meta, vm_meta, _, kf_meta, vf_meta, c_meta = _projections(
        meta_pad, pad_tab(cos), pad_tab(sin), jnp.zeros((1, LANES), F32), consts, tm=META_PAD)
    c0 = c_meta[0, N_META - 1:N_META, :]

    qm, km, vm, qf, kf, vf, _ = _projections(x, cos[N_META:], sin[N_META:], c0, consts, tm=t_attn)
    ya_t = _attention(qm, km, vm, km_meta, vm_meta, t=ATTN_QUERIES, tk=t_attn, g=ATTN_HEADS_PER_STEP)
    yb_t = _attention(qf, kf, vf, kf_meta, vf_meta, t=ATTN_QUERIES, tk=t_attn, g=ATTN_HEADS_PER_STEP)

    w_r = jnp.concatenate([w_expert_router[0], w_group_router[0],
                           jnp.zeros((d, LANES - N_EXPERTS - N_GROUPS), F32)], axis=1)
    w_rh = w_r.astype(BF16)
    w_rhl = jnp.concatenate([w_rh, (w_r - w_rh.astype(F32)).astype(BF16)], axis=1)
    b_r = jnp.pad(jnp.concatenate([row(b_expert_router[0]), row(b_group_router[0])], axis=1),
                  ((0, 0), (0, LANES - N_EXPERTS - N_GROUPS)))
    h1, t_rows, ri, rw, cnt = _merge(
        x, ya_t, yb_t,
        [row(attn_norm[0]), w_gates, w_mla_out[0].astype(BF16), w_fox_out[0].astype(BF16),
         w_out[0].astype(BF16), row(ffn_norm[0]), w_rhl, b_r], tm=512)

    n = b * seq
    counts = cnt[:N_EXPERTS, 0].astype(jnp.int32)
    row_start, padded, tiles_used, tile_expert, tile_valid, nt = _tile_plan(counts, n, tm=EXPERT_ROWS)
    dst = _dst_rows(row_start, ri)
    xs = _dispatch(row_start, counts, padded, tiles_used, dst, t_rows, nt, tm=512, te=EXPERT_ROWS)
    y_rows = _experts(tile_expert, tile_valid, xs,
                      w_gate[0].reshape(N_EXPERTS, d, D_EXPERT), w_up[0].reshape(N_EXPERTS, d, D_EXPERT),
                      w_down[0].reshape(N_EXPERTS, D_EXPERT, d), tm=EXPERT_ROWS)
    out = _combine(dst, y_rows.reshape(-1, ROW_TILE, LANES), h1.reshape(n, d), rw.reshape(n, LANES),
                   row(final_norm), tc=COMBINE_TOKENS)
    return out.reshape(b, seq, d)
```

```python
import functools

import numpy as np
import jax
import jax.numpy as jnp
from jax import lax
from jax.experimental import pallas as pl
from jax.experimental.pallas import tpu as pltpu

F32 = jnp.float32
BF16 = jnp.bfloat16

D_MODEL = 1024
N_META = 16
HEADS = 8
MLA_NOPE, MLA_ROPE, MLA_V = 64, 32, 64
Q_LORA, KV_LORA = 512, 256
ROPE_THETA = 10000.0
FOX_HD = 64
N_GROUPS, EXPERTS_PER_GROUP, D_EXPERT = 4, 8, 256
N_EXPERTS = N_GROUPS * EXPERTS_PER_GROUP
EPS = 1e-6
NEG = -1e30
LOG2E = 1.4426950408889634

LANES = 128
HEAD_PAD = LANES
META_PAD = LANES
ROPE_LO = MLA_NOPE
ROPE_HALF = MLA_ROPE // 2
FOX_AUG = FOX_HD
N_SPLIT = 3

VMEM_LIMIT = 56 * 1024 * 1024
ATTN_QUERIES = 1024
ATTN_HEADS_PER_STEP = 8
SKEW = 2
ACC_PAD = 16
ROW_TILE = D_MODEL // LANES
ROW_PITCH = ROW_TILE + 1
IDX_FIELDS = 4
X_AHEAD = 3
EXPERT_ROWS = 256
COMBINE_TOKENS = 512

_NT = (((1,), (1,)), ((), ()))
_TN = (((0,), (0,)), ((), ()))
_TT = (((0,), (1,)), ((), ()))


def _rms(x, g):
    return x * lax.rsqrt(jnp.mean(x * x, axis=-1, keepdims=True) + EPS) * g


def _rows_to_tiles(ref2d, v, pitch=ROW_TILE):
    n = v.shape[0]
    for c in range(ROW_TILE):
        ref2d[pl.ds(c, n, stride=pitch), :] = v[:, c * LANES:(c + 1) * LANES]
    for c in range(ROW_TILE, pitch):
        ref2d[pl.ds(c, n, stride=pitch), :] = jnp.zeros((n, LANES), v.dtype)


def _rows_from_tiles(ref2d, first, n, pitch):
    return jnp.concatenate([ref2d[pl.ds(first + c, n, stride=pitch), :] for c in range(ROW_TILE)], axis=1)


def _split3(v):
    hi = v.astype(BF16)
    r = v - hi.astype(F32)
    mid = r.astype(BF16)
    lo = (r - mid.astype(F32)).astype(BF16)
    return hi, mid, lo


def _rope_tile(v, c, s):
    lane = lax.broadcasted_iota(jnp.int32, v.shape, 1)
    partner = jnp.where(lane < ROPE_LO + ROPE_HALF,
                        pltpu.roll(v, LANES - ROPE_HALF, 1), pltpu.roll(v, ROPE_HALF, 1))
    return v * c + partner * s


def _proj_kernel(x_ref, cos_ref, sin_ref, c0_ref, an_ref, wa_ref, qn_ref, wq_ref, kvn_ref, wk_ref,
                 wv_ref, wfv_ref, bf_ref, eq_ref, ek_ref, oq_ref, ok_ref,
                 qm_ref, km_ref, vm_ref, qf_ref, kf_ref, vf_ref, cl_ref, carry_ref, *, tm):
    i = pl.program_id(1)

    @pl.when(i == 0)
    def _():
        carry_ref[...] = jnp.broadcast_to(c0_ref[...], carry_ref.shape)

    ub = _rms(x_ref[0], an_ref[...]).astype(BF16)
    pa = jnp.dot(ub, wa_ref[...], preferred_element_type=F32)
    c_q = pa[:, 0:Q_LORA]
    c_kv = pa[:, Q_LORA:Q_LORA + KV_LORA]
    o = Q_LORA + KV_LORA
    kr = pa[:, o:o + LANES]
    fl = pa[:, o + LANES:o + 2 * LANES]
    o += 2 * LANES
    fq = pa[:, o:o + HEADS * FOX_HD]
    fk = pa[:, o + HEADS * FOX_HD:o + 2 * HEADS * FOX_HD]

    cos = cos_ref[...]
    sin = sin_ref[...]

    cqn = _rms(c_q, qn_ref[...]).astype(BF16)
    q = jnp.dot(cqn, wq_ref[...], preferred_element_type=F32)
    ckvn = _rms(c_kv, kvn_ref[...]).astype(BF16)
    kb = jnp.dot(ckvn, wk_ref[...], preferred_element_type=F32)
    krr = _rope_tile(kr, cos, sin)
    q_scale = (MLA_NOPE + MLA_ROPE) ** -0.5 * LOG2E
    lane = lax.broadcasted_iota(jnp.int32, krr.shape, 1)
    for h in range(HEADS):
        sl = slice(h * HEAD_PAD, (h + 1) * HEAD_PAD)
        qm_ref[0, h] = (_rope_tile(q[:, sl], cos, sin) * q_scale).astype(BF16)
        kh = kb[:, (h // 2) * LANES:(h // 2 + 1) * LANES]
        if h % 2:
            kh = pltpu.roll(kh, LANES - MLA_NOPE, 1)
        km_ref[0, h] = jnp.where(lane < MLA_NOPE, kh, krr).astype(BF16)
    vt = lax.dot_general(wv_ref[...], ckvn, _TT, preferred_element_type=F32)
    for h in range(HEADS):
        vm_ref[0, h, 0] = vt[h * MLA_V:(h + 1) * MLA_V, :].astype(BF16)

    z = fl + bf_ref[...]
    logf = jnp.minimum(z, 0.0) - jnp.log1p(jnp.exp(-jnp.abs(z)))
    r_i = lax.broadcasted_iota(jnp.int32, (tm, tm), 0)
    c_i = lax.broadcasted_iota(jnp.int32, (tm, tm), 1)
    tri = jnp.where(c_i <= r_i, 1.0, 0.0).astype(BF16)
    hi, mid, lo = _split3(logf)
    cs2 = jnp.dot(tri, jnp.concatenate([hi, mid], axis=1), preferred_element_type=F32)
    cs = cs2[:, :LANES] + cs2[:, LANES:] + jnp.dot(tri, lo, preferred_element_type=F32)
    c = carry_ref[0:1, :] + cs
    carry_ref[...] = jnp.broadcast_to(c[tm - 1:tm, :], carry_ref.shape)
    cl_ref[0] = c
    p_hi, p_mid, p_lo = (piece.astype(F32) for piece in _split3(c * LOG2E))
    cat = jnp.where(lane < HEADS, p_hi,
                    jnp.where(lane < 2 * HEADS, pltpu.roll(p_mid, HEADS, 1),
                              jnp.where(lane < 3 * HEADS, pltpu.roll(p_lo, 2 * HEADS, 1), 0.0))).astype(BF16)
    augq = jnp.dot(cat, eq_ref[...], preferred_element_type=F32) + oq_ref[...]
    augk = jnp.dot(cat, ek_ref[...], preferred_element_type=F32) + ok_ref[...]
    fqs = fq * (FOX_HD ** -0.5 * LOG2E)
    for h in range(HEADS):
        pair = slice((h // 2) * LANES, (h // 2 + 1) * LANES)
        sl = slice(h * HEAD_PAD, (h + 1) * HEAD_PAD)
        qh, kh = fqs[:, pair], fk[:, pair]
        if h % 2:
            qh, kh = pltpu.roll(qh, LANES - FOX_HD, 1), pltpu.roll(kh, LANES - FOX_HD, 1)
        qf_ref[0, h] = jnp.where(lane < FOX_HD, qh, augq[:, sl]).astype(BF16)
        kf_ref[0, h] = jnp.where(lane < FOX_HD, kh, augk[:, sl]).astype(BF16)
    vft = lax.dot_general(wfv_ref[...], ub, _TT, preferred_element_type=F32)
    for h in range(HEADS):
        vf_ref[0, h, 0] = vft[h * FOX_HD:(h + 1) * FOX_HD, :].astype(BF16)


def _projections(x3, cos, sin, c0, consts, *, tm):
    b, l, _ = x3.shape
    nt = l // tm
    full = lambda a: pl.BlockSpec(a.shape, lambda bi, i: (0,) * a.ndim)
    qk_shape = jax.ShapeDtypeStruct((b, HEADS, l, HEAD_PAD), BF16)
    vt_shape = jax.ShapeDtypeStruct((b, HEADS, nt, MLA_V, tm), BF16)
    qk_spec = pl.BlockSpec((1, HEADS, tm, HEAD_PAD), lambda bi, i: (bi, 0, i, 0))
    vt_spec = pl.BlockSpec((1, HEADS, 1, MLA_V, tm), lambda bi, i: (bi, 0, i, 0, 0))
    return pl.pallas_call(
        functools.partial(_proj_kernel, tm=tm),
        grid=(b, nt),
        in_specs=[pl.BlockSpec((1, tm, D_MODEL), lambda bi, i: (bi, i, 0)),
                  pl.BlockSpec((tm, LANES), lambda bi, i: (i, 0)),
                  pl.BlockSpec((tm, LANES), lambda bi, i: (i, 0)),
                  full(c0)] + [full(a) for a in consts],
        out_specs=[qk_spec, qk_spec, vt_spec, qk_spec, qk_spec, vt_spec,
                   pl.BlockSpec((1, tm, LANES), lambda bi, i: (bi, i, 0))],
        out_shape=[qk_shape, qk_shape, vt_shape, qk_shape, qk_shape, vt_shape,
                   jax.ShapeDtypeStruct((b, l, LANES), F32)],
        scratch_shapes=[pltpu.VMEM((8, LANES), F32)],
        compiler_params=pltpu.CompilerParams(
            dimension_semantics=("arbitrary", "arbitrary"), vmem_limit_bytes=VMEM_LIMIT),
        name="proj",
    )(x3, cos, sin, c0, *consts)


def _attn_kernel(q_ref, k_ref, vt_ref, km_ref, vtm_ref, o_ref, m_ref, acc_ref, *, t, tk, g):
    i = pl.program_id(2)
    dv = vt_ref.shape[3]

    def ones_rows(n):
        return jnp.where(lax.broadcasted_iota(jnp.int32, (ACC_PAD, n), 0) == 0, 1.0, 0.0).astype(BF16)

    def update(hd, s, vt_blk, lo):
        m_prev = m_ref[hd, :, lo:]
        m_new = jnp.maximum(m_prev, jnp.max(s, axis=0, keepdims=True))
        alpha = jnp.exp2(m_prev - m_new)
        p = jnp.exp2(s - m_new).astype(BF16)
        vt_aug = jnp.concatenate([vt_blk, ones_rows(p.shape[0])], axis=0)
        pv = jnp.dot(vt_aug, p, preferred_element_type=F32)
        acc_ref[hd, :, lo:] = alpha * acc_ref[hd, :, lo:] + pv
        m_ref[hd, :, lo:] = m_new

    def run(items):
        def scores(n):
            hd, k_of, _, _, lo = items[n]
            return lax.dot_general(k_of(), q_ref[0, hd, lo:, :], _NT, preferred_element_type=F32)
        pending = [scores(n) for n in range(SKEW)]
        for n, (hd, _, vt_of, mask, lo) in enumerate(items):
            if n + SKEW < len(items):
                pending.append(scores(n + SKEW))
            s = pending.pop(0)
            if mask is not None:
                s = jnp.where(mask, s, NEG)
            update(hd, s, vt_of(), lo)

    def block_items(j, mask=None, lo=0):
        off = pl.multiple_of(j * tk, tk)
        return [(hd, lambda hd=hd: k_ref[0, hd, pl.ds(off, tk), :], lambda hd=hd: vt_ref[0, hd, j], mask, lo)
                for hd in range(g)]

    m_ref[...] = jnp.full(m_ref.shape, NEG, F32)
    acc_ref[...] = jnp.zeros(acc_ref.shape, F32)

    per = t // tk

    def body(jj, carry):
        run([it for d in range(per) for it in block_items(jj * per + d)])
        return carry

    lax.fori_loop(0, i, body, 0)

    tail = []
    for d in range(per):
        key_row = lax.broadcasted_iota(jnp.int32, (tk, t - d * tk), 0)
        query = lax.broadcasted_iota(jnp.int32, (tk, t - d * tk), 1)
        tail += block_items(i * per + d, key_row <= query, d * tk)
    row = lax.broadcasted_iota(jnp.int32, (META_PAD, t), 0)
    tail += [(hd, lambda hd=hd: km_ref[0, hd], lambda hd=hd: vtm_ref[0, hd, 0], row < N_META, 0)
             for hd in range(g)]
    run(tail)

    for hd in range(g):
        a = acc_ref[hd]
        o_ref[0, hd * dv:(hd + 1) * dv, :] = (a[:dv] / a[dv:dv + 1]).astype(o_ref.dtype)


def _attention(q, k, vt, k_meta, vt_meta, *, t, tk, g):
    b, h, l, _ = q.shape
    nq = l // t
    dv = vt.shape[3]
    return pl.pallas_call(
        functools.partial(_attn_kernel, t=t, tk=tk, g=g),
        grid=(b, h // g, nq),
        in_specs=[pl.BlockSpec((1, g, t, HEAD_PAD), lambda bi, hi, i: (bi, hi, i, 0)),
                  pl.BlockSpec((1, g, l, HEAD_PAD), lambda bi, hi, i: (bi, hi, 0, 0),
                               pipeline_mode=pl.Buffered(1)),
                  pl.BlockSpec((1, g, l // tk, dv, tk), lambda bi, hi, i: (bi, hi, 0, 0, 0),
                               pipeline_mode=pl.Buffered(1)),
                  pl.BlockSpec((1, g, META_PAD, HEAD_PAD), lambda bi, hi, i: (0, hi, 0, 0)),
                  pl.BlockSpec((1, g, 1, dv, META_PAD), lambda bi, hi, i: (0, hi, 0, 0, 0))],
        out_specs=pl.BlockSpec((1, g * dv, t), lambda bi, hi, i: (bi, hi, i)),
        out_shape=jax.ShapeDtypeStruct((b, h * dv, l), BF16),
        scratch_shapes=[pltpu.VMEM((g, 1, t), F32), pltpu.VMEM((g, dv + ACC_PAD, t), F32)],
        compiler_params=pltpu.CompilerParams(
            dimension_semantics=("arbitrary", "arbitrary", "arbitrary"), vmem_limit_bytes=VMEM_LIMIT),
        name="attn",
    )(q, k, vt, k_meta, vt_meta)


def _merge_kernel(x_ref, ya_ref, yb_ref, an_ref, wg_ref, wmo_ref, wfo_ref, wo_ref, fn_ref,
                  wrhl_ref, br_ref, h1_ref, t_ref, ri_ref, rw_ref, cnt_ref):
    @pl.when((pl.program_id(0) == 0) & (pl.program_id(1) == 0))
    def _():
        cnt_ref[...] = jnp.zeros_like(cnt_ref)

    x = x_ref[0]
    ub = _rms(x, an_ref[...]).astype(BF16)
    g = jnp.dot(ub, wg_ref[...], preferred_element_type=F32)
    ya = lax.dot_general(ya_ref[0], wmo_ref[...], _TN, preferred_element_type=F32)
    yb = lax.dot_general(yb_ref[0], wfo_ref[...], _TN, preferred_element_type=F32)
    merged = jax.nn.sigmoid(g[:, :D_MODEL]) * ya + jax.nn.sigmoid(g[:, D_MODEL:]) * yb
    h1 = x + jnp.dot(merged.astype(BF16), wo_ref[...], preferred_element_type=F32)
    h1_ref[0] = h1
    t = _rms(h1, fn_ref[...])
    _rows_to_tiles(t_ref, t, ROW_PITCH)

    th = t.astype(BF16)
    tl = (t - th.astype(F32)).astype(BF16)
    hl = jnp.dot(th, wrhl_ref[...], preferred_element_type=F32)
    logits = (hl[:, :LANES] + hl[:, LANES:]
              + jnp.dot(tl, wrhl_ref[:, :LANES], preferred_element_type=F32)) + br_ref[...]
    tm = logits.shape[0]
    lt = logits.T
    row = lax.broadcasted_iota(jnp.int32, lt.shape, 0)
    big = jnp.int32(LANES)

    def first_argmax(v, vmax):
        return jnp.min(jnp.where(v == vmax, row, big), axis=0, keepdims=True)

    gmask = (row >= N_EXPERTS) & (row < N_EXPERTS + N_GROUPS)
    gl = jnp.where(gmask, lt, NEG)
    gmax = jnp.max(gl, axis=0, keepdims=True)
    g_w = 1.0 / jnp.sum(jnp.where(gmask, jnp.exp(gl - gmax), 0.0), axis=0, keepdims=True)
    g_idx = first_argmax(gl, gmax) - N_EXPERTS
    emask = (row < N_EXPERTS) & (jnp.right_shift(row, 3) == g_idx)
    el = jnp.where(emask, lt, NEG)
    m1 = jnp.max(el, axis=0, keepdims=True)
    i1 = first_argmax(el, m1)
    el2 = jnp.where(row == i1, NEG, el)
    m2 = jnp.max(el2, axis=0, keepdims=True)
    i2 = first_argmax(el2, m2)
    zsum = jnp.sum(jnp.where(emask, jnp.exp(el - m1), 0.0), axis=0, keepdims=True)
    p1 = 1.0 / zsum
    p2 = jnp.exp(m2 - m1) / zsum
    tot = p1 + p2
    w1 = g_w * (p1 / tot)
    w2 = g_w * (p2 / tot)
    rw_ref[0] = jnp.where(row == 0, w1, jnp.where(row == 1, w2, 0.0)).T

    hot1 = row == i1
    hot2 = row == i2
    onehot = jnp.where(hot1, 1.0, jnp.where(hot2, 1.0, 0.0)).astype(BF16)
    c_i = lax.broadcasted_iota(jnp.int32, (tm, tm), 0)
    r_i = lax.broadcasted_iota(jnp.int32, (tm, tm), 1)
    earlier = jnp.where(c_i < r_i, 1.0, 0.0).astype(BF16)
    cnt = cnt_ref[...]
    rank = (jnp.concatenate([cnt] * (tm // LANES), axis=1)
            + jnp.dot(onehot, earlier, preferred_element_type=F32))
    rank1 = jnp.sum(jnp.where(hot1, rank, 0.0), axis=0, keepdims=True).astype(jnp.int32)
    rank2 = jnp.sum(jnp.where(hot2, rank, 0.0), axis=0, keepdims=True).astype(jnp.int32)
    field = lax.broadcasted_iota(jnp.int32, (8, tm), 0)
    fields = jnp.where(field == 0, i1, jnp.where(field == 1, i2,
                       jnp.where(field == 2, rank1, jnp.where(field == 3, rank2, 0))))
    ri_ref[0] = fields[0:IDX_FIELDS, :]
    cnt_ref[...] = cnt + jnp.dot(onehot, jnp.ones((tm, LANES), BF16), preferred_element_type=F32)


def _merge(x3, ya_t, yb_t, consts, *, tm):
    b, l, _ = x3.shape
    full = lambda a: pl.BlockSpec(a.shape, lambda bi, i: (0,) * a.ndim)
    tok = lambda w: pl.BlockSpec((1, tm, w), lambda bi, i: (bi, i, 0))
    yspec = pl.BlockSpec((1, ya_t.shape[1], tm), lambda bi, i: (bi, 0, i))
    return pl.pallas_call(
        _merge_kernel,
        grid=(b, l // tm),
        in_specs=[tok(D_MODEL), yspec, yspec] + [full(a) for a in consts],
        out_specs=[tok(D_MODEL),
                   pl.BlockSpec((tm * ROW_PITCH, LANES), lambda bi, i: (bi * (l // tm) + i, 0)),
                   pl.BlockSpec((1, IDX_FIELDS, tm), lambda bi, i: (bi * (l // tm) + i, 0, 0)), tok(LANES),
                   pl.BlockSpec((LANES, LANES), lambda bi, i: (0, 0))],
        out_shape=[jax.ShapeDtypeStruct((b, l, D_MODEL), F32),
                   jax.ShapeDtypeStruct((b * l * ROW_PITCH, LANES), F32),
                   jax.ShapeDtypeStruct((b * (l // tm), IDX_FIELDS, tm), jnp.int32),
                   jax.ShapeDtypeStruct((b, l, LANES), F32),
                   jax.ShapeDtypeStruct((LANES, LANES), F32)],
        compiler_params=pltpu.CompilerParams(
            dimension_semantics=("arbitrary", "arbitrary"), vmem_limit_bytes=VMEM_LIMIT),
        name="merge",
    )(x3, ya_t, yb_t, *consts)


def _dst_rows_kernel(rs_ref, idx_ref, d_ref):
    for k in range(2):
        e = idx_ref[:, k, :]
        start = jnp.zeros_like(e)
        for j in range(N_EXPERTS):
            start = jnp.where(e == j, rs_ref[j], start)
        d_ref[:, k, :] = start + idx_ref[:, 2 + k, :]


def _dst_rows(row_start, idx):
    nt, _, tm = idx.shape
    return pl.pallas_call(
        _dst_rows_kernel,
        grid_spec=pltpu.PrefetchScalarGridSpec(
            num_scalar_prefetch=1, grid=(1,),
            in_specs=[pl.BlockSpec(idx.shape, lambda i, rs: (0, 0, 0))],
            out_specs=pl.BlockSpec((nt, 2, tm), lambda i, rs: (0, 0, 0))),
        out_shape=jax.ShapeDtypeStruct((nt, 2, tm), jnp.int32),
        name="dst_rows",
    )(row_start, idx)


def _dispatch_kernel(rs_ref, cnt_ref, pad_ref, used_ref, d_ref, t_ref, xs_hbm, zero_ref, sem, zsem, *,
                     tm, te, nt):
    def row_tile(row):
        return xs_hbm.at[pl.ds(row * ROW_PITCH, ROW_PITCH)]

    def zero_rows(first, n):
        return pltpu.make_async_copy(zero_ref.at[pl.ds(0, n * ROW_PITCH)],
                                     xs_hbm.at[pl.ds(first * ROW_PITCH, n * ROW_PITCH)], zsem)

    def pad_copies(act):
        for e in range(N_EXPERTS):
            n_pad = pad_ref[e] - cnt_ref[e]
            first = rs_ref[e] + cnt_ref[e]
            size = te // 2
            while size:
                @pl.when((n_pad & size) != 0)
                def _(size=size, n_pad=n_pad, first=first):
                    act(zero_rows(first + (n_pad & ~(2 * size - 1)), size))
                size //= 2

    def tail_copies(act):
        def tile(k, carry):
            act(zero_rows(k * te, te))
            return carry
        lax.fori_loop(used_ref[0], nt, tile, 0)

    @pl.when(pl.program_id(0) == 0)
    def _():
        zero_ref[...] = jnp.zeros_like(zero_ref)
        pad_copies(lambda cp: cp.start())
        tail_copies(lambda cp: cp.start())
        pad_copies(lambda cp: cp.wait())
        tail_copies(lambda cp: cp.wait())

    def copy(r, k):
        return pltpu.make_async_copy(t_ref.at[pl.ds(r * ROW_PITCH, ROW_PITCH)], row_tile(d_ref[0, k, r]), sem)

    def row(r, carry):
        copy(r, 0).start()
        copy(r, 1).start(priority=1)
        return carry
    lax.fori_loop(0, tm, row, 0, unroll=8)
    for _ in range(2 * tm):
        copy(0, 0).wait()


def _dispatch(row_start, counts, padded, tiles_used, dst, t_rows, n_tiles, *, tm, te):
    grid_spec = pltpu.PrefetchScalarGridSpec(
        num_scalar_prefetch=4,
        grid=(dst.shape[0],),
        in_specs=[pl.BlockSpec((1, 2, tm), lambda i, *_: (i, 0, 0), memory_space=pltpu.SMEM),
                  pl.BlockSpec((tm * ROW_PITCH, LANES), lambda i, *_: (i, 0))],
        out_specs=pl.BlockSpec(memory_space=pl.ANY),
        scratch_shapes=[pltpu.VMEM((te * ROW_PITCH, LANES), F32), pltpu.SemaphoreType.DMA(()),
                        pltpu.SemaphoreType.DMA(())])
    return pl.pallas_call(
        functools.partial(_dispatch_kernel, tm=tm, te=te, nt=n_tiles),
        grid_spec=grid_spec,
        out_shape=jax.ShapeDtypeStruct((n_tiles * te * ROW_PITCH, LANES), F32),
        compiler_params=pltpu.CompilerParams(dimension_semantics=("arbitrary",), vmem_limit_bytes=VMEM_LIMIT),
        name="dispatch",
    )(row_start, counts, padded, tiles_used, dst, t_rows)


def _expert_kernel(te_ref, tv_ref, xs_hbm, wg_ref, wu_ref, wd_ref, y_ref, xbuf, sem, *, tm, nt):
    del te_ref
    i = pl.program_id(0)
    rows = tm * ROW_PITCH

    def fetch(k):
        slot = lax.rem(k, X_AHEAD + 1)
        return pltpu.make_async_copy(xs_hbm.at[pl.ds(k * rows, rows)], xbuf.at[slot], sem.at[slot])

    @pl.when(i == 0)
    def _():
        for k in range(X_AHEAD):
            @pl.when(tv_ref[k] > 0)
            def _():
                fetch(k).start()

    ahead = jnp.minimum(i + X_AHEAD, nt - 1)

    @pl.when((i + X_AHEAD < nt) & (tv_ref[ahead] > 0))
    def _():
        fetch(ahead).start()

    @pl.when(tv_ref[i] > 0)
    def _():
        fetch(i).wait()
        x = _rows_from_tiles(xbuf.at[lax.rem(i, X_AHEAD + 1)], 0, tm, ROW_PITCH).astype(BF16)
        a = jnp.dot(x, wg_ref[0].astype(BF16), preferred_element_type=F32)
        u = jnp.dot(x, wu_ref[0].astype(BF16), preferred_element_type=F32)
        hm = (a * jax.nn.sigmoid(a)) * u
        _rows_to_tiles(y_ref, jnp.dot(hm.astype(BF16), wd_ref[0].astype(BF16), preferred_element_type=F32))

    @pl.when(tv_ref[i] == 0)
    def _():
        y_ref[...] = jnp.zeros_like(y_ref)


def _experts(tile_expert, tile_valid, xs, w_gate, w_up, w_down, *, tm):
    nt = tile_expert.shape[0]
    wspec = lambda shape: pl.BlockSpec((1,) + shape, lambda i, te, tv: (te[i], 0, 0))
    grid_spec = pltpu.PrefetchScalarGridSpec(
        num_scalar_prefetch=2,
        grid=(nt,),
        in_specs=[pl.BlockSpec(memory_space=pl.ANY),
                  wspec((D_MODEL, D_EXPERT)), wspec((D_MODEL, D_EXPERT)), wspec((D_EXPERT, D_MODEL))],
        out_specs=pl.BlockSpec((tm * ROW_TILE, LANES), lambda i, te, tv: (i, 0)),
        scratch_shapes=[pltpu.VMEM((X_AHEAD + 1, tm * ROW_PITCH, LANES), F32),
                        pltpu.SemaphoreType.DMA((X_AHEAD + 1,))])
    return pl.pallas_call(
        functools.partial(_expert_kernel, tm=tm, nt=nt),
        grid_spec=grid_spec,
        out_shape=jax.ShapeDtypeStruct((nt * tm * ROW_TILE, LANES), F32),
        compiler_params=pltpu.CompilerParams(dimension_semantics=("arbitrary",), vmem_limit_bytes=VMEM_LIMIT),
        name="experts",
    )(tile_expert, tile_valid, xs, w_gate, w_up, w_down)


def _combine_kernel(d_ref, d_next_ref, y_hbm, h1_ref, rw_ref, fn_ref, o_ref, ybuf, sem, *, tc, nt):
    i = pl.program_id(0)
    slot = lax.rem(i, 2)
    half = 2 * tc * ROW_PITCH

    def copy(dr, first, s, r, k):
        return pltpu.make_async_copy(y_hbm.at[dr[0, k, r]],
                                     ybuf.at[pl.ds(first + (k * tc + r) * ROW_PITCH, ROW_TILE)], sem.at[s])

    def gather(dr, first, s):
        def row(r, carry):
            copy(dr, first, s, r, 0).start()
            copy(dr, first, s, r, 1).start(priority=1)
            return carry
        lax.fori_loop(0, tc, row, 0, unroll=8)

    @pl.when(i == 0)
    def _():
        gather(d_ref, 0, 0)

    @pl.when(i + 1 < nt)
    def _():
        gather(d_next_ref, (1 - slot) * half, 1 - slot)

    for _ in range(2 * tc):
        copy(d_ref, slot * half, slot, 0, 0).wait()
    rw = rw_ref[...]
    y = (rw[:, 0:1] * _rows_from_tiles(ybuf, slot * half, tc, ROW_PITCH)
         + rw[:, 1:2] * _rows_from_tiles(ybuf, slot * half + tc * ROW_PITCH, tc, ROW_PITCH))
    o_ref[...] = _rms(h1_ref[...] + y, fn_ref[...])


def _combine(dst, y_rows, h2, rw2, final_norm, *, tc):
    n = h2.shape[0]
    nt = n // tc
    per = dst.shape[2] // tc
    blk = lambda i: (i // per, 0, lax.rem(i, per))
    return pl.pallas_call(
        functools.partial(_combine_kernel, tc=tc, nt=nt),
        grid=(nt,),
        in_specs=[pl.BlockSpec((1, 2, tc), lambda i: blk(i), memory_space=pltpu.SMEM),
                  pl.BlockSpec((1, 2, tc), lambda i: blk(jnp.minimum(i + 1, nt - 1)), memory_space=pltpu.SMEM),
                  pl.BlockSpec(memory_space=pl.ANY),
                  pl.BlockSpec((tc, D_MODEL), lambda i: (i, 0)),
                  pl.BlockSpec((tc, LANES), lambda i: (i, 0)),
                  pl.BlockSpec((1, D_MODEL), lambda i: (0, 0))],
        out_specs=pl.BlockSpec((tc, D_MODEL), lambda i: (i, 0)),
        out_shape=jax.ShapeDtypeStruct((n, D_MODEL), F32),
        scratch_shapes=[pltpu.VMEM((2 * 2 * tc * ROW_PITCH, LANES), F32), pltpu.SemaphoreType.DMA((2,))],
        compiler_params=pltpu.CompilerParams(dimension_semantics=("arbitrary",), vmem_limit_bytes=VMEM_LIMIT),
        name="combine",
    )(dst, dst, y_rows, h2, rw2, final_norm)


def _tile_plan(counts, n, *, tm):
    nt = (2 * n) // tm + N_EXPERTS
    tiles_e = (counts + tm - 1) // tm
    tile_end = jnp.cumsum(tiles_e)
    tile_start = tile_end - tiles_e
    k = jnp.arange(nt, dtype=jnp.int32)
    tile_expert = jnp.minimum(jnp.sum((k[:, None] >= tile_end[None, :]).astype(jnp.int32), axis=1), N_EXPERTS - 1)
    onehot = (tile_expert[:, None] == jnp.arange(N_EXPERTS)[None, :]).astype(jnp.int32)
    left = jnp.sum(onehot * (counts[None, :] - (k[:, None] - tile_start[None, :]) * tm), axis=1)
    tile_valid = jnp.where(k < tile_end[-1], jnp.clip(left, 0, tm), 0).astype(jnp.int32)
    return ((tile_start * tm).astype(jnp.int32), (tiles_e * tm).astype(jnp.int32),
            tile_end[-1:].astype(jnp.int32), tile_expert.astype(jnp.int32), tile_valid, nt)


def _pad_heads(w, width):
    k = w.shape[0]
    return jnp.pad(w.reshape(k, HEADS, width), ((0, 0), (0, 0), (0, HEAD_PAD - width))).reshape(k, HEADS * HEAD_PAD)


def _rope_tables(length):
    pos = np.arange(length, dtype=np.float64)
    inv = ROPE_THETA ** (-np.arange(0, MLA_ROPE, 2, dtype=np.float64) / MLA_ROPE)
    ang = pos[:, None] * inv[None, :]
    cos, sin = np.cos(ang), np.sin(ang)
    one = np.ones((length, ROPE_LO))
    zero_lo = np.zeros((length, ROPE_LO))
    zero_hi = np.zeros((length, LANES - ROPE_LO - MLA_ROPE))
    return (np.concatenate([one, cos, cos, zero_hi], axis=1).astype(np.float32),
            np.concatenate([zero_lo, -sin, sin, zero_hi], axis=1).astype(np.float32))


def _decay_columns():
    eq = np.zeros((LANES, HEADS * HEAD_PAD), np.float32)
    ek = np.zeros_like(eq)
    oq = np.zeros((1, HEADS * HEAD_PAD), np.float32)
    ok = np.zeros_like(oq)
    for h in range(HEADS):
        for p in range(N_SPLIT):
            eq[p * HEADS + h, h * HEAD_PAD + FOX_AUG + p] = 1.0
            ek[p * HEADS + h, h * HEAD_PAD + FOX_AUG + N_SPLIT + p] = -1.0
            oq[0, h * HEAD_PAD + FOX_AUG + N_SPLIT + p] = 1.0
            ok[0, h * HEAD_PAD + FOX_AUG + p] = 1.0
    return jnp.asarray(eq, BF16), jnp.asarray(ek, BF16), jnp.asarray(oq), jnp.asarray(ok)


def kernel(x, meta, attn_norm, w_in, b_forget, q_a_norm, w_q_up, kv_a_norm, w_kv_up, w_mla_out, w_fox_out, w_out, ffn_norm, w_group_router, b_group_router, w_expert_router, b_expert_router, w_gate, w_up, w_down, final_norm):
    b, seq, d = x.shape
    t_attn = 512
    row = lambda v: v.reshape(1, -1).astype(F32)

    w = w_in[0]
    offs = np.cumsum([0, Q_LORA, KV_LORA, MLA_ROPE, HEADS * FOX_HD, HEADS * FOX_HD, HEADS * FOX_HD,
                      HEADS, D_MODEL, D_MODEL])
    seg = lambda j: w[:, offs[j]:offs[j + 1]]
    zcol = lambda n: jnp.zeros((d, n), F32)
    w_a = jnp.concatenate(
        [seg(0), seg(1),
         zcol(ROPE_LO), seg(2), zcol(LANES - ROPE_LO - MLA_ROPE),
         seg(6), zcol(LANES - HEADS),
         seg(3), seg(4)], axis=1).astype(BF16)
    w_fv = seg(5).astype(BF16)
    w_gates = jnp.concatenate([seg(7), seg(8)], axis=1).astype(BF16)
    w_q = _pad_heads(w_q_up[0], MLA_NOPE + MLA_ROPE).astype(BF16)
    kv3 = w_kv_up[0].reshape(KV_LORA, HEADS, MLA_NOPE + MLA_V)
    w_k = kv3[:, :, :MLA_NOPE].reshape(KV_LORA, HEADS * MLA_NOPE).astype(BF16)
    w_v = kv3[:, :, MLA_NOPE:].reshape(KV_LORA, HEADS * MLA_V).astype(BF16)
    bf128 = jnp.pad(row(b_forget[0]), ((0, 0), (0, LANES - HEADS)))
    eq, ek, oq, ok = _decay_columns()
    cos, sin = _rope_tables(N_META + seq)
    consts = [row(attn_norm[0]), w_a, row(q_a_norm[0]), w_q, row(kv_a_norm[0]), w_k, w_v, w_fv, bf128,
              eq, ek, oq, ok]

    meta_pad = jnp.pad(meta.astype(F32), ((0, META_PAD - N_META), (0, 0)))[None]
    pad_tab = lambda tab: np.pad(tab[:N_META], ((0, META_PAD - N_META), (0, 0)))
    _, km_meta, vm_meta, _, kf_meta, vf_meta, c_meta = _projections(
        meta_pad, pad_tab(cos), pad_tab(sin), jnp.zeros((1, LANES), F32), consts, tm=META_PAD)
    c0 = c_meta[0, N_META - 1:N_META, :]

    qm, km, vm, qf, kf, vf, _ = _projections(x, cos[N_META:], sin[N_META:], c0, consts, tm=t_attn)
    ya_t = _attention(qm, km, vm, km_meta, vm_meta, t=ATTN_QUERIES, tk=t_attn, g=ATTN_HEADS_PER_STEP)
    yb_t = _attention(qf, kf, vf, kf_meta, vf_meta, t=ATTN_QUERIES, tk=t_attn, g=ATTN_HEADS_PER_STEP)

    w_r = jnp.concatenate([w_expert_router[0], w_group_router[0],
                           jnp.zeros((d, LANES - N_EXPERTS - N_GROUPS), F32)], axis=1)
    w_rh = w_r.astype(BF16)
    w_rhl = jnp.concatenate([w_rh, (w_r - w_rh.astype(F32)).astype(BF16)], axis=1)
    b_r = jnp.pad(jnp.concatenate([row(b_expert_router[0]), row(b_group_router[0])], axis=1),
                  ((0, 0), (0, LANES - N_EXPERTS - N_GROUPS)))
    h1, t_rows, ri, rw, cnt = _merge(
        x, ya_t, yb_t,
        [row(attn_norm[0]), w_gates, w_mla_out[0].astype(BF16), w_fox_out[0].astype(BF16),
         w_out[0].astype(BF16), row(ffn_norm[0]), w_rhl, b_r], tm=512)

    n = b * seq
    counts = cnt[:N_EXPERTS, 0].astype(jnp.int32)
    row_start, padded, tiles_used, tile_expert, tile_valid, nt = _tile_plan(counts, n, tm=EXPERT_ROWS)
    dst = _dst_rows(row_start, ri)
    xs = _dispatch(row_start, counts, padded, tiles_used, dst, t_rows, nt, tm=512, te=EXPERT_ROWS)
    y_rows = _experts(tile_expert, tile_valid, xs,
                      w_gate[0].reshape(N_EXPERTS, d, D_EXPERT), w_up[0].reshape(N_EXPERTS, d, D_EXPERT),
                      w_down[0].reshape(N_EXPERTS, D_EXPERT, d), tm=EXPERT_ROWS)
    out = _combine(dst, y_rows.reshape(-1, ROW_TILE, LANES), h1.reshape(n, d), rw.reshape(n, LANES),
                   row(final_norm), tc=COMBINE_TOKENS)
    return out.reshape(b, seq, d)
```

```python
import functools

import numpy as np
import jax
import jax.numpy as jnp
from jax import lax
from jax.experimental import pallas as pl
from jax.experimental.pallas import tpu as pltpu

F32 = jnp.float32
BF16 = jnp.bfloat16

D_MODEL = 1024
N_META = 16
HEADS = 8
MLA_NOPE, MLA_ROPE, MLA_V = 64, 32, 64
Q_LORA, KV_LORA = 512, 256
ROPE_THETA = 10000.0
FOX_HD = 64
N_GROUPS, EXPERTS_PER_GROUP, D_EXPERT = 4, 8, 256
N_EXPERTS = N_GROUPS * EXPERTS_PER_GROUP
EPS = 1e-6
NEG = -1e30
LOG2E = 1.4426950408889634

LANES = 128
HEAD_PAD = LANES
META_PAD = LANES
ROPE_LO = MLA_NOPE
ROPE_HALF = MLA_ROPE // 2
FOX_AUG = FOX_HD
N_SPLIT = 3

VMEM_LIMIT = 56 * 1024 * 1024
ATTN_QUERIES = 1024
ATTN_HEADS_PER_STEP = 8
SKEW = 2
ACC_PAD = 16
ROW_TILE = D_MODEL // LANES
ROW_PITCH = ROW_TILE + 1
IDX_FIELDS = 4
X_AHEAD = 3
EXPERT_ROWS = 256
COMBINE_TOKENS = 512

_NT = (((1,), (1,)), ((), ()))
_TN = (((0,), (0,)), ((), ()))
_TT = (((0,), (1,)), ((), ()))


def _rms(x, g):
    return x * lax.rsqrt(jnp.mean(x * x, axis=-1, keepdims=True) + EPS) * g


def _rows_to_tiles(ref2d, v, pitch=ROW_TILE):
    n = v.shape[0]
    for c in range(ROW_TILE):
        ref2d[pl.ds(c, n, stride=pitch), :] = v[:, c * LANES:(c + 1) * LANES]
    for c in range(ROW_TILE, pitch):
        ref2d[pl.ds(c, n, stride=pitch), :] = jnp.zeros((n, LANES), v.dtype)


def _rows_from_tiles(ref2d, first, n, pitch):
    return jnp.concatenate([ref2d[pl.ds(first + c, n, stride=pitch), :] for c in range(ROW_TILE)], axis=1)


def _split3(v):
    hi = v.astype(BF16)
    r = v - hi.astype(F32)
    mid = r.astype(BF16)
    lo = (r - mid.astype(F32)).astype(BF16)
    return hi, mid, lo


def _rope_tile(v, c, s):
    lane = lax.broadcasted_iota(jnp.int32, v.shape, 1)
    partner = jnp.where(lane < ROPE_LO + ROPE_HALF,
                        pltpu.roll(v, LANES - ROPE_HALF, 1), pltpu.roll(v, ROPE_HALF, 1))
    return v * c + partner * s


def _proj_kernel(x_ref, cos_ref, sin_ref, c0_ref, an_ref, wa_ref, qn_ref, wq_ref, kvn_ref, wk_ref,
                 wv_ref, wfv_ref, bf_ref, eq_ref, ek_ref, oq_ref, ok_ref,
                 qm_ref, km_ref, vm_ref, qf_ref, kf_ref, vf_ref, cl_ref, carry_ref, *, tm):
    i = pl.program_id(1)

    @pl.when(i == 0)
    def _():
        carry_ref[...] = jnp.broadcast_to(c0_ref[...], carry_ref.shape)

    ub = _rms(x_ref[0], an_ref[...]).astype(BF16)
    pa = jnp.dot(ub, wa_ref[...], preferred_element_type=F32)
    c_q = pa[:, 0:Q_LORA]
    c_kv = pa[:, Q_LORA:Q_LORA + KV_LORA]
    o = Q_LORA + KV_LORA
    kr = pa[:, o:o + LANES]
    fl = pa[:, o + LANES:o + 2 * LANES]
    o += 2 * LANES
    fq = pa[:, o:o + HEADS * FOX_HD]
    fk = pa[:, o + HEADS * FOX_HD:o + 2 * HEADS * FOX_HD]

    cos = cos_ref[...]
    sin = sin_ref[...]

    cqn = _rms(c_q, qn_ref[...]).astype(BF16)
    q = jnp.dot(cqn, wq_ref[...], preferred_element_type=F32)
    ckvn = _rms(c_kv, kvn_ref[...]).astype(BF16)
    kb = jnp.dot(ckvn, wk_ref[...], preferred_element_type=F32)
    krr = _rope_tile(kr, cos, sin)
    q_scale = (MLA_NOPE + MLA_ROPE) ** -0.5 * LOG2E
    lane = lax.broadcasted_iota(jnp.int32, krr.shape, 1)
    for h in range(HEADS):
        sl = slice(h * HEAD_PAD, (h + 1) * HEAD_PAD)
        qm_ref[0, h] = (_rope_tile(q[:, sl], cos, sin) * q_scale).astype(BF16)
        kh = kb[:, (h // 2) * LANES:(h // 2 + 1) * LANES]
        if h % 2:
            kh = pltpu.roll(kh, LANES - MLA_NOPE, 1)
        km_ref[0, h] = jnp.where(lane < MLA_NOPE, kh, krr).astype(BF16)
    vt = lax.dot_general(wv_ref[...], ckvn, _TT, preferred_element_type=F32)
    for h in range(HEADS):
        vm_ref[0, h, 0] = vt[h * MLA_V:(h + 1) * MLA_V, :].astype(BF16)

    z = fl + bf_ref[...]
    logf = jnp.minimum(z, 0.0) - jnp.log1p(jnp.exp(-jnp.abs(z)))
    r_i = lax.broadcasted_iota(jnp.int32, (tm, tm), 0)
    c_i = lax.broadcasted_iota(jnp.int32, (tm, tm), 1)
    tri = jnp.where(c_i <= r_i, 1.0, 0.0).astype(BF16)
    hi, mid, lo = _split3(logf)
    cs2 = jnp.dot(tri, jnp.concatenate([hi, mid], axis=1), preferred_element_type=F32)
    cs = cs2[:, :LANES] + cs2[:, LANES:] + jnp.dot(tri, lo, preferred_element_type=F32)
    c = carry_ref[0:1, :] + cs
    carry_ref[...] = jnp.broadcast_to(c[tm - 1:tm, :], carry_ref.shape)
    cl_ref[0] = c
    p_hi, p_mid, p_lo = (piece.astype(F32) for piece in _split3(c * LOG2E))
    cat = jnp.where(lane < HEADS, p_hi,
                    jnp.where(lane < 2 * HEADS, pltpu.roll(p_mid, HEADS, 1),
                              jnp.where(lane < 3 * HEADS, pltpu.roll(p_lo, 2 * HEADS, 1), 0.0))).astype(BF16)
    augq = jnp.dot(cat, eq_ref[...], preferred_element_type=F32) + oq_ref[...]
    augk = jnp.dot(cat, ek_ref[...], preferred_element_type=F32) + ok_ref[...]
    fqs = fq * (FOX_HD ** -0.5 * LOG2E)
    for h in range(HEADS):
        pair = slice((h // 2) * LANES, (h // 2 + 1) * LANES)
        sl = slice(h * HEAD_PAD, (h + 1) * HEAD_PAD)
        qh, kh = fqs[:, pair], fk[:, pair]
        if h % 2:
            qh, kh = pltpu.roll(qh, LANES - FOX_HD, 1), pltpu.roll(kh, LANES - FOX_HD, 1)
        qf_ref[0, h] = jnp.where(lane < FOX_HD, qh, augq[:, sl]).astype(BF16)
        kf_ref[0, h] = jnp.where(lane < FOX_HD, kh, augk[:, sl]).astype(BF16)
    vft = lax.dot_general(wfv_ref[...], ub, _TT, preferred_element_type=F32)
    for h in range(HEADS):
        vf_ref[0, h, 0] = vft[h * FOX_HD:(h + 1) * FOX_HD, :].astype(BF16)


def _projections(x3, cos, sin, c0, consts, *, tm):
    b, l, _ = x3.shape
    nt = l // tm
    full = lambda a: pl.BlockSpec(a.shape, lambda bi, i: (0,) * a.ndim)
    qk_shape = jax.ShapeDtypeStruct((b, HEADS, l, HEAD_PAD), BF16)
    vt_shape = jax.ShapeDtypeStruct((b, HEADS, nt, MLA_V, tm), BF16)
    qk_spec = pl.BlockSpec((1, HEADS, tm, HEAD_PAD), lambda bi, i: (bi, 0, i, 0))
    vt_spec = pl.BlockSpec((1, HEADS, 1, MLA_V, tm), lambda bi, i: (bi, 0, i, 0, 0))
    return pl.pallas_call(
        functools.partial(_proj_kernel, tm=tm),
        grid=(b, nt),
        in_specs=[pl.BlockSpec((1, tm, D_MODEL), lambda bi, i: (bi, i, 0)),
                  pl.BlockSpec((tm, LANES), lambda bi, i: (i, 0)),
                  pl.BlockSpec((tm, LANES), lambda bi, i: (i, 0)),
                  full(c0)] + [full(a) for a in consts],
        out_specs=[qk_spec, qk_spec, vt_spec, qk_spec, qk_spec, vt_spec,
                   pl.BlockSpec((1, tm, LANES), lambda bi, i: (bi, i, 0))],
        out_shape=[qk_shape, qk_shape, vt_shape, qk_shape, qk_shape, vt_shape,
                   jax.ShapeDtypeStruct((b, l, LANES), F32)],
        scratch_shapes=[pltpu.VMEM((8, LANES), F32)],
        compiler_params=pltpu.CompilerParams(
            dimension_semantics=("arbitrary", "arbitrary"), vmem_limit_bytes=VMEM_LIMIT),
        name="proj",
    )(x3, cos, sin, c0, *consts)


def _attn_kernel(q_ref, k_ref, vt_ref, km_ref, vtm_ref, o_ref, m_ref, acc_ref, *, t, tk, g):
    i = pl.program_id(2)
    dv = vt_ref.shape[3]

    def ones_rows(n):
        return jnp.where(lax.broadcasted_iota(jnp.int32, (ACC_PAD, n), 0) == 0, 1.0, 0.0).astype(BF16)

    def update(hd, s, vt_blk, lo):
        m_prev = m_ref[hd, :, lo:]
        m_new = jnp.maximum(m_prev, jnp.max(s, axis=0, keepdims=True))
        alpha = jnp.exp2(m_prev - m_new)
        p = jnp.exp2(s - m_new).astype(BF16)
        vt_aug = jnp.concatenate([vt_blk, ones_rows(p.shape[0])], axis=0)
        pv = jnp.dot(vt_aug, p, preferred_element_type=F32)
        acc_ref[hd, :, lo:] = alpha * acc_ref[hd, :, lo:] + pv
        m_ref[hd, :, lo:] = m_new

    def run(items):
        def scores(n):
            hd, k_of, _, _, lo = items[n]
            return lax.dot_general(k_of(), q_ref[0, hd, lo:, :], _NT, preferred_element_type=F32)
        pending = [scores(n) for n in range(SKEW)]
        for n, (hd, _, vt_of, mask, lo) in enumerate(items):
            if n + SKEW < len(items):
                pending.append(scores(n + SKEW))
            s = pending.pop(0)
            if mask is not None:
                s = jnp.where(mask, s, NEG)
            update(hd, s, vt_of(), lo)

    def block_items(j, mask=None, lo=0):
        off = pl.multiple_of(j * tk, tk)
        return [(hd, lambda hd=hd: k_ref[0, hd, pl.ds(off, tk), :], lambda hd=hd: vt_ref[0, hd, j], mask, lo)
                for hd in range(g)]

    m_ref[...] = jnp.full(m_ref.shape, NEG, F32)
    acc_ref[...] = jnp.zeros(acc_ref.shape, F32)

    per = t // tk

    def body(jj, carry):
        run([it for d in range(per) for it in block_items(jj * per + d)])
        return carry

    lax.fori_loop(0, i, body, 0)

    tail = []
    for d in range(per):
        key_row = lax.broadcasted_iota(jnp.int32, (tk, t - d * tk), 0)
        query = lax.broadcasted_iota(jnp.int32, (tk, t - d * tk), 1)
        tail += block_items(i * per + d, key_row <= query, d * tk)
    row = lax.broadcasted_iota(jnp.int32, (META_PAD, t), 0)
    tail += [(hd, lambda hd=hd: km_ref[0, hd], lambda hd=hd: vtm_ref[0, hd, 0], row < N_META, 0)
             for hd in range(g)]
    run(tail)

    for hd in range(g):
        a = acc_ref[hd]
        o_ref[0, hd * dv:(hd + 1) * dv, :] = (a[:dv] / a[dv:dv + 1]).astype(o_ref.dtype)


def _attention(q, k, vt, k_meta, vt_meta, *, t, tk, g):
    b, h, l, _ = q.shape
    nq = l // t
    dv = vt.shape[3]
    return pl.pallas_call(
        functools.partial(_attn_kernel, t=t, tk=tk, g=g),
        grid=(b, h // g, nq),
        in_specs=[pl.BlockSpec((1, g, t, HEAD_PAD), lambda bi, hi, i: (bi, hi, i, 0)),
                  pl.BlockSpec((1, g, l, HEAD_PAD), lambda bi, hi, i: (bi, hi, 0, 0),
                               pipeline_mode=pl.Buffered(1)),
                  pl.BlockSpec((1, g, l // tk, dv, tk), lambda bi, hi, i: (bi, hi, 0, 0, 0),
                               pipeline_mode=pl.Buffered(1)),
                  pl.BlockSpec((1, g, META_PAD, HEAD_PAD), lambda bi, hi, i: (0, hi, 0, 0)),
                  pl.BlockSpec((1, g, 1, dv, META_PAD), lambda bi, hi, i: (0, hi, 0, 0, 0))],
        out_specs=pl.BlockSpec((1, g * dv, t), lambda bi, hi, i: (bi, hi, i)),
        out_shape=jax.ShapeDtypeStruct((b, h * dv, l), BF16),
        scratch_shapes=[pltpu.VMEM((g, 1, t), F32), pltpu.VMEM((g, dv + ACC_PAD, t), F32)],
        compiler_params=pltpu.CompilerParams(
            dimension_semantics=("arbitrary", "arbitrary", "arbitrary"), vmem_limit_bytes=VMEM_LIMIT),
        name="attn",
    )(q, k, vt, k_meta, vt_meta)


def _merge_kernel(x_ref, ya_ref, yb_ref, an_ref, wg_ref, wmo_ref, wfo_ref, wo_ref, fn_ref,
                  wrhl_ref, br_ref, h1_ref, t_ref, ri_ref, rw_ref, cnt_ref):
    @pl.when((pl.program_id(0) == 0) & (pl.program_id(1) == 0))
    def _():
        cnt_ref[...] = jnp.zeros_like(cnt_ref)

    x = x_ref[0]
    ub = _rms(x, an_ref[...]).astype(BF16)
    g = jnp.dot(ub, wg_ref[...], preferred_element_type=F32)
    ya = lax.dot_general(ya_ref[0], wmo_ref[...], _TN, preferred_element_type=F32)
    yb = lax.dot_general(yb_ref[0], wfo_ref[...], _TN, preferred_element_type=F32)
    merged = jax.nn.sigmoid(g[:, :D_MODEL]) * ya + jax.nn.sigmoid(g[:, D_MODEL:]) * yb
    h1 = x + jnp.dot(merged.astype(BF16), wo_ref[...], preferred_element_type=F32)
    h1_ref[0] = h1
    t = _rms(h1, fn_ref[...])
    _rows_to_tiles(t_ref, t, ROW_PITCH)

    th = t.astype(BF16)
    tl = (t - th.astype(F32)).astype(BF16)
    hl = jnp.dot(th, wrhl_ref[...], preferred_element_type=F32)
    logits = (hl[:, :LANES] + hl[:, LANES:]
              + jnp.dot(tl, wrhl_ref[:, :LANES], preferred_element_type=F32)) + br_ref[...]
    tm = logits.shape[0]
    lt = logits.T
    row = lax.broadcasted_iota(jnp.int32, lt.shape, 0)
    big = jnp.int32(LANES)

    def first_argmax(v, vmax):
        return jnp.min(jnp.where(v == vmax, row, big), axis=0, keepdims=True)

    gmask = (row >= N_EXPERTS) & (row < N_EXPERTS + N_GROUPS)
    gl = jnp.where(gmask, lt, NEG)
    gmax = jnp.max(gl, axis=0, keepdims=True)
    g_w = 1.0 / jnp.sum(jnp.where(gmask, jnp.exp(gl - gmax), 0.0), axis=0, keepdims=True)
    g_idx = first_argmax(gl, gmax) - N_EXPERTS
    emask = (row < N_EXPERTS) & (jnp.right_shift(row, 3) == g_idx)
    el = jnp.where(emask, lt, NEG)
    m1 = jnp.max(el, axis=0, keepdims=True)
    i1 = first_argmax(el, m1)
    el2 = jnp.where(row == i1, NEG, el)
    m2 = jnp.max(el2, axis=0, keepdims=True)
    i2 = first_argmax(el2, m2)
    zsum = jnp.sum(jnp.where(emask, jnp.exp(el - m1), 0.0), axis=0, keepdims=True)
    p1 = 1.0 / zsum
    p2 = jnp.exp(m2 - m1) / zsum
    tot = p1 + p2
    w1 = g_w * (p1 / tot)
    w2 = g_w * (p2 / tot)
    rw_ref[0] = jnp.where(row == 0, w1, jnp.where(row == 1, w2, 0.0)).T

    hot1 = row == i1
    hot2 = row == i2
    onehot = jnp.where(hot1, 1.0, jnp.where(hot2, 1.0, 0.0)).astype(BF16)
    c_i = lax.broadcasted_iota(jnp.int32, (tm, tm), 0)
    r_i = lax.broadcasted_iota(jnp.int32, (tm, tm), 1)
    earlier = jnp.where(c_i < r_i, 1.0, 0.0).astype(BF16)
    cnt = cnt_ref[...]
    rank = (jnp.concatenate([cnt] * (tm // LANES), axis=1)
            + jnp.dot(onehot, earlier, preferred_element_type=F32))
    rank1 = jnp.sum(jnp.where(hot1, rank, 0.0), axis=0, keepdims=True).astype(jnp.int32)
    rank2 = jnp.sum(jnp.where(hot2, rank, 0.0), axis=0, keepdims=True).astype(jnp.int32)
    field = lax.broadcasted_iota(jnp.int32, (8, tm), 0)
    fields = jnp.where(field == 0, i1, jnp.where(field == 1, i2,
                       jnp.where(field == 2, rank1, jnp.where(field == 3, rank2, 0))))
    ri_ref[0] = fields[0:IDX_FIELDS, :]
    cnt_ref[...] = cnt + jnp.dot(onehot, jnp.ones((tm, LANES), BF16), preferred_element_type=F32)


def _merge(x3, ya_t, yb_t, consts, *, tm):
    b, l, _ = x3.shape
    full = lambda a: pl.BlockSpec(a.shape, lambda bi, i: (0,) * a.ndim)
    tok = lambda w: pl.BlockSpec((1, tm, w), lambda bi, i: (bi, i, 0))
    yspec = pl.BlockSpec((1, ya_t.shape[1], tm), lambda bi, i: (bi, 0, i))
    return pl.pallas_call(
        _merge_kernel,
        grid=(b, l // tm),
        in_specs=[tok(D_MODEL), yspec, yspec] + [full(a) for a in consts],
        out_specs=[tok(D_MODEL),
                   pl.BlockSpec((tm * ROW_PITCH, LANES), lambda bi, i: (bi * (l // tm) + i, 0)),
                   pl.BlockSpec((1, IDX_FIELDS, tm), lambda bi, i: (bi * (l // tm) + i, 0, 0)), tok(LANES),
                   pl.BlockSpec((LANES, LANES), lambda bi, i: (0, 0))],
        out_shape=[jax.ShapeDtypeStruct((b, l, D_MODEL), F32),
                   jax.ShapeDtypeStruct((b * l * ROW_PITCH, LANES), F32),
                   jax.ShapeDtypeStruct((b * (l // tm), IDX_FIELDS, tm), jnp.int32),
                   jax.ShapeDtypeStruct((b, l, LANES), F32),
                   jax.ShapeDtypeStruct((LANES, LANES), F32)],
        compiler_params=pltpu.CompilerParams(
            dimension_semantics=("arbitrary", "arbitrary"), vmem_limit_bytes=VMEM_LIMIT),
        name="merge",
    )(x3, ya_t, yb_t, *consts)


def _dst_rows_kernel(rs_ref, idx_ref, d_ref):
    for k in range(2):
        e = idx_ref[:, k, :]
        start = jnp.zeros_like(e)
        for j in range(N_EXPERTS):
            start = jnp.where(e == j, rs_ref[j], start)
        d_ref[:, k, :] = start + idx_ref[:, 2 + k, :]


def _dst_rows(row_start, idx):
    nt, _, tm = idx.shape
    return pl.pallas_call(
        _dst_rows_kernel,
        grid_spec=pltpu.PrefetchScalarGridSpec(
            num_scalar_prefetch=1, grid=(1,),
            in_specs=[pl.BlockSpec(idx.shape, lambda i, rs: (0, 0, 0))],
            out_specs=pl.BlockSpec((nt, 2, tm), lambda i, rs: (0, 0, 0))),
        out_shape=jax.ShapeDtypeStruct((nt, 2, tm), jnp.int32),
        name="dst_rows",
    )(row_start, idx)


def _dispatch_kernel(rs_ref, cnt_ref, pad_ref, used_ref, d_ref, t_ref, xs_hbm, zero_ref, sem, zsem, *,
                     tm, te, nt):
    def row_tile(row):
        return xs_hbm.at[pl.ds(row * ROW_PITCH, ROW_PITCH)]

    def zero_rows(first, n):
        return pltpu.make_async_copy(zero_ref.at[pl.ds(0, n * ROW_PITCH)],
                                     xs_hbm.at[pl.ds(first * ROW_PITCH, n * ROW_PITCH)], zsem)

    def pad_copies(act):
        for e in range(N_EXPERTS):
            n_pad = pad_ref[e] - cnt_ref[e]
            first = rs_ref[e] + cnt_ref[e]
            size = te // 2
            while size:
                @pl.when((n_pad & size) != 0)
                def _(size=size, n_pad=n_pad, first=first):
                    act(zero_rows(first + (n_pad & ~(2 * size - 1)), size))
                size //= 2

    def tail_copies(act):
        def tile(k, carry):
            act(zero_rows(k * te, te))
            return carry
        lax.fori_loop(used_ref[0], nt, tile, 0)

    @pl.when(pl.program_id(0) == 0)
    def _():
        zero_ref[...] = jnp.zeros_like(zero_ref)
        pad_copies(lambda cp: cp.start())
        tail_copies(lambda cp: cp.start())
        pad_copies(lambda cp: cp.wait())
        tail_copies(lambda cp: cp.wait())

    def copy(r, k):
        return pltpu.make_async_copy(t_ref.at[pl.ds(r * ROW_PITCH, ROW_PITCH)], row_tile(d_ref[0, k, r]), sem)

    def row(r, carry):
        copy(r, 0).start()
        copy(r, 1).start(priority=1)
        return carry
    lax.fori_loop(0, tm, row, 0, unroll=8)
    for _ in range(2 * tm):
        copy(0, 0).wait()


def _dispatch(row_start, counts, padded, tiles_used, dst, t_rows, n_tiles, *, tm, te):
    grid_spec = pltpu.PrefetchScalarGridSpec(
        num_scalar_prefetch=4,
        grid=(dst.shape[0],),
        in_specs=[pl.BlockSpec((1, 2, tm), lambda i, *_: (i, 0, 0), memory_space=pltpu.SMEM),
                  pl.BlockSpec((tm * ROW_PITCH, LANES), lambda i, *_: (i, 0))],
        out_specs=pl.BlockSpec(memory_space=pl.ANY),
        scratch_shapes=[pltpu.VMEM((te * ROW_PITCH, LANES), F32), pltpu.SemaphoreType.DMA(()),
                        pltpu.SemaphoreType.DMA(())])
    return pl.pallas_call(
        functools.partial(_dispatch_kernel, tm=tm, te=te, nt=n_tiles),
        grid_spec=grid_spec,
        out_shape=jax.ShapeDtypeStruct((n_tiles * te * ROW_PITCH, LANES), F32),
        compiler_params=pltpu.CompilerParams(dimension_semantics=("arbitrary",), vmem_limit_bytes=VMEM_LIMIT),
        name="dispatch",
    )(row_start, counts, padded, tiles_used, dst, t_rows)


def _expert_kernel(te_ref, tv_ref, xs_hbm, wg_ref, wu_ref, wd_ref, y_ref, xbuf, sem, wgb, wub, wdb, *, tm, nt):
    i = pl.program_id(0)
    rows = tm * ROW_PITCH

    @pl.when((i == 0) | (te_ref[i] != te_ref[jnp.maximum(i - 1, 0)]))
    def _():
        wgb[...] = wg_ref[0].astype(BF16)
        wub[...] = wu_ref[0].astype(BF16)
        wdb[...] = wd_ref[0].astype(BF16)

    def fetch(k):
        slot = lax.rem(k, X_AHEAD + 1)
        return pltpu.make_async_copy(xs_hbm.at[pl.ds(k * rows, rows)], xbuf.at[slot], sem.at[slot])

    @pl.when(i == 0)
    def _():
        for k in range(X_AHEAD):
            @pl.when(tv_ref[k] > 0)
            def _():
                fetch(k).start()

    ahead = jnp.minimum(i + X_AHEAD, nt - 1)

    @pl.when((i + X_AHEAD < nt) & (tv_ref[ahead] > 0))
    def _():
        fetch(ahead).start()

    @pl.when(tv_ref[i] > 0)
    def _():
        fetch(i).wait()
        x = _rows_from_tiles(xbuf.at[lax.rem(i, X_AHEAD + 1)], 0, tm, ROW_PITCH).astype(BF16)
        a = jnp.dot(x, wgb[...], preferred_element_type=F32)
        u = jnp.dot(x, wub[...], preferred_element_type=F32)
        hm = (a * jax.nn.sigmoid(a)) * u
        _rows_to_tiles(y_ref, jnp.dot(hm.astype(BF16), wdb[...], preferred_element_type=F32))

    @pl.when(tv_ref[i] == 0)
    def _():
        y_ref[...] = jnp.zeros_like(y_ref)


def _experts(tile_expert, tile_valid, xs, w_gate, w_up, w_down, *, tm):
    nt = tile_expert.shape[0]
    wspec = lambda shape: pl.BlockSpec((1,) + shape, lambda i, te, tv: (te[i], 0, 0))
    grid_spec = pltpu.PrefetchScalarGridSpec(
        num_scalar_prefetch=2,
        grid=(nt,),
        in_specs=[pl.BlockSpec(memory_space=pl.ANY),
                  wspec((D_MODEL, D_EXPERT)), wspec((D_MODEL, D_EXPERT)), wspec((D_EXPERT, D_MODEL))],
        out_specs=pl.BlockSpec((tm * ROW_TILE, LANES), lambda i, te, tv: (i, 0)),
        scratch_shapes=[pltpu.VMEM((X_AHEAD + 1, tm * ROW_PITCH, LANES), F32),
                        pltpu.SemaphoreType.DMA((X_AHEAD + 1,)),
                        pltpu.VMEM((D_MODEL, D_EXPERT), BF16), pltpu.VMEM((D_MODEL, D_EXPERT), BF16),
                        pltpu.VMEM((D_EXPERT, D_MODEL), BF16)])
    return pl.pallas_call(
        functools.partial(_expert_kernel, tm=tm, nt=nt),
        grid_spec=grid_spec,
        out_shape=jax.ShapeDtypeStruct((nt * tm * ROW_TILE, LANES), F32),
        compiler_params=pltpu.CompilerParams(dimension_semantics=("arbitrary",), vmem_limit_bytes=VMEM_LIMIT),
        name="experts",
    )(tile_expert, tile_valid, xs, w_gate, w_up, w_down)


def _combine_kernel(d_ref, d_next_ref, y_hbm, h1_ref, rw_ref, fn_ref, o_ref, ybuf, sem, *, tc, nt):
    i = pl.program_id(0)
    slot = lax.rem(i, 2)
    half = 2 * tc * ROW_PITCH

    def copy(dr, first, s, r, k):
        return pltpu.make_async_copy(y_hbm.at[dr[0, k, r]],
                                     ybuf.at[pl.ds(first + (k * tc + r) * ROW_PITCH, ROW_TILE)], sem.at[s])

    def gather(dr, first, s):
        def row(r, carry):
            copy(dr, first, s, r, 0).start()
            copy(dr, first, s, r, 1).start(priority=1)
            return carry
        lax.fori_loop(0, tc, row, 0, unroll=8)

    @pl.when(i == 0)
    def _():
        gather(d_ref, 0, 0)

    @pl.when(i + 1 < nt)
    def _():
        gather(d_next_ref, (1 - slot) * half, 1 - slot)

    for _ in range(2 * tc):
        copy(d_ref, slot * half, slot, 0, 0).wait()
    rw = rw_ref[...]
    y = (rw[:, 0:1] * _rows_from_tiles(ybuf, slot * half, tc, ROW_PITCH)
         + rw[:, 1:2] * _rows_from_tiles(ybuf, slot * half + tc * ROW_PITCH, tc, ROW_PITCH))
    o_ref[...] = _rms(h1_ref[...] + y, fn_ref[...])


def _combine(dst, y_rows, h2, rw2, final_norm, *, tc):
    n = h2.shape[0]
    nt = n // tc
    per = dst.shape[2] // tc
    blk = lambda i: (i // per, 0, lax.rem(i, per))
    return pl.pallas_call(
        functools.partial(_combine_kernel, tc=tc, nt=nt),
        grid=(nt,),
        in_specs=[pl.BlockSpec((1, 2, tc), lambda i: blk(i), memory_space=pltpu.SMEM),
                  pl.BlockSpec((1, 2, tc), lambda i: blk(jnp.minimum(i + 1, nt - 1)), memory_space=pltpu.SMEM),
                  pl.BlockSpec(memory_space=pl.ANY),
                  pl.BlockSpec((tc, D_MODEL), lambda i: (i, 0)),
                  pl.BlockSpec((tc, LANES), lambda i: (i, 0)),
                  pl.BlockSpec((1, D_MODEL), lambda i: (0, 0))],
        out_specs=pl.BlockSpec((tc, D_MODEL), lambda i: (i, 0)),
        out_shape=jax.ShapeDtypeStruct((n, D_MODEL), F32),
        scratch_shapes=[pltpu.VMEM((2 * 2 * tc * ROW_PITCH, LANES), F32), pltpu.SemaphoreType.DMA((2,))],
        compiler_params=pltpu.CompilerParams(dimension_semantics=("arbitrary",), vmem_limit_bytes=VMEM_LIMIT),
        name="combine",
    )(dst, dst, y_rows, h2, rw2, final_norm)


def _tile_plan(counts, n, *, tm):
    nt = (2 * n) // tm + N_EXPERTS
    tiles_e = (counts + tm - 1) // tm
    tile_end = jnp.cumsum(tiles_e)
    tile_start = tile_end - tiles_e
    k = jnp.arange(nt, dtype=jnp.int32)
    tile_expert = jnp.minimum(jnp.sum((k[:, None] >= tile_end[None, :]).astype(jnp.int32), axis=1), N_EXPERTS - 1)
    onehot = (tile_expert[:, None] == jnp.arange(N_EXPERTS)[None, :]).astype(jnp.int32)
    left = jnp.sum(onehot * (counts[None, :] - (k[:, None] - tile_start[None, :]) * tm), axis=1)
    tile_valid = jnp.where(k < tile_end[-1], jnp.clip(left, 0, tm), 0).astype(jnp.int32)
    return ((tile_start * tm).astype(jnp.int32), (tiles_e * tm).astype(jnp.int32),
            tile_end[-1:].astype(jnp.int32), tile_expert.astype(jnp.int32), tile_valid, nt)


def _pad_heads(w, width):
    k = w.shape[0]
    return jnp.pad(w.reshape(k, HEADS, width), ((0, 0), (0, 0), (0, HEAD_PAD - width))).reshape(k, HEADS * HEAD_PAD)


def _rope_tables(length):
    pos = np.arange(length, dtype=np.float64)
    inv = ROPE_THETA ** (-np.arange(0, MLA_ROPE, 2, dtype=np.float64) / MLA_ROPE)
    ang = pos[:, None] * inv[None, :]
    cos, sin = np.cos(ang), np.sin(ang)
    one = np.ones((length, ROPE_LO))
    zero_lo = np.zeros((length, ROPE_LO))
    zero_hi = np.zeros((length, LANES - ROPE_LO - MLA_ROPE))
    return (np.concatenate([one, cos, cos, zero_hi], axis=1).astype(np.float32),
            np.concatenate([zero_lo, -sin, sin, zero_hi], axis=1).astype(np.float32))


def _decay_columns():
    eq = np.zeros((LANES, HEADS * HEAD_PAD), np.float32)
    ek = np.zeros_like(eq)
    oq = np.zeros((1, HEADS * HEAD_PAD), np.float32)
    ok = np.zeros_like(oq)
    for h in range(HEADS):
        for p in range(N_SPLIT):
            eq[p * HEADS + h, h * HEAD_PAD + FOX_AUG + p] = 1.0
            ek[p * HEADS + h, h * HEAD_PAD + FOX_AUG + N_SPLIT + p] = -1.0
            oq[0, h * HEAD_PAD + FOX_AUG + N_SPLIT + p] = 1.0
            ok[0, h * HEAD_PAD + FOX_AUG + p] = 1.0
    return jnp.asarray(eq, BF16), jnp.asarray(ek, BF16), jnp.asarray(oq), jnp.asarray(ok)


def kernel(x, meta, attn_norm, w_in, b_forget, q_a_norm, w_q_up, kv_a_norm, w_kv_up, w_mla_out, w_fox_out, w_out, ffn_norm, w_group_router, b_group_router, w_expert_router, b_expert_router, w_gate, w_up, w_down, final_norm):
    b, seq, d = x.shape
    t_attn = 512
    row = lambda v: v.reshape(1, -1).astype(F32)

    w = w_in[0]
    offs = np.cumsum([0, Q_LORA, KV_LORA, MLA_ROPE, HEADS * FOX_HD, HEADS * FOX_HD, HEADS * FOX_HD,
                      HEADS, D_MODEL, D_MODEL])
    seg = lambda j: w[:, offs[j]:offs[j + 1]]
    zcol = lambda n: jnp.zeros((d, n), F32)
    w_a = jnp.concatenate(
        [seg(0), seg(1),
         zcol(ROPE_LO), seg(2), zcol(LANES - ROPE_LO - MLA_ROPE),
         seg(6), zcol(LANES - HEADS),
         seg(3), seg(4)], axis=1).astype(BF16)
    w_fv = seg(5).astype(BF16)
    w_gates = jnp.concatenate([seg(7), seg(8)], axis=1).astype(BF16)
    w_q = _pad_heads(w_q_up[0], MLA_NOPE + MLA_ROPE).astype(BF16)
    kv3 = w_kv_up[0].reshape(KV_LORA, HEADS, MLA_NOPE + MLA_V)
    w_k = kv3[:, :, :MLA_NOPE].reshape(KV_LORA, HEADS * MLA_NOPE).astype(BF16)
    w_v = kv3[:, :, MLA_NOPE:].reshape(KV_LORA, HEADS * MLA_V).astype(BF16)
    bf128 = jnp.pad(row(b_forget[0]), ((0, 0), (0, LANES - HEADS)))
    eq, ek, oq, ok = _decay_columns()
    cos, sin = _rope_tables(N_META + seq)
    consts = [row(attn_norm[0]), w_a, row(q_a_norm[0]), w_q, row(kv_a_norm[0]), w_k, w_v, w_fv, bf128,
              eq, ek, oq, ok]

    meta_pad = jnp.pad(meta.astype(F32), ((0, META_PAD - N_META), (0, 0)))[None]
    pad_tab = lambda tab: np.pad(tab[:N_META], ((0, META_PAD - N_META), (0, 0)))
    _, km_meta, vm_meta, _, kf_meta, vf_meta, c_meta = _projections(
        meta_pad, pad_tab(cos), pad_tab(sin), jnp.zeros((1, LANES), F32), consts, tm=META_PAD)
    c0 = c_meta[0, N_META - 1:N_META, :]

    qm, km, vm, qf, kf, vf, _ = _projections(x, cos[N_META:], sin[N_META:], c0, consts, tm=t_attn)
    ya_t = _attention(qm, km, vm, km_meta, vm_meta, t=ATTN_QUERIES, tk=t_attn, g=ATTN_HEADS_PER_STEP)
    yb_t = _attention(qf, kf, vf, kf_meta, vf_meta, t=ATTN_QUERIES, tk=t_attn, g=ATTN_HEADS_PER_STEP)

    w_r = jnp.concatenate([w_expert_router[0], w_group_router[0],
                           jnp.zeros((d, LANES - N_EXPERTS - N_GROUPS), F32)], axis=1)
    w_rh = w_r.astype(BF16)
    w_rhl = jnp.concatenate([w_rh, (w_r - w_rh.astype(F32)).astype(BF16)], axis=1)
    b_r = jnp.pad(jnp.concatenate([row(b_expert_router[0]), row(b_group_router[0])], axis=1),
                  ((0, 0), (0, LANES - N_EXPERTS - N_GROUPS)))
    h1, t_rows, ri, rw, cnt = _merge(
        x, ya_t, yb_t,
        [row(attn_norm[0]), w_gates, w_mla_out[0].astype(BF16), w_fox_out[0].astype(BF16),
         w_out[0].astype(BF16), row(ffn_norm[0]), w_rhl, b_r], tm=512)

    n = b * seq
    counts = cnt[:N_EXPERTS, 0].astype(jnp.int32)
    row_start, padded, tiles_used, tile_expert, tile_valid, nt = _tile_plan(counts, n, tm=EXPERT_ROWS)
    dst = _dst_rows(row_start, ri)
    xs = _dispatch(row_start, counts, padded, tiles_used, dst, t_rows, nt, tm=512, te=EXPERT_ROWS)
    y_rows = _experts(tile_expert, tile_valid, xs,
                      w_gate[0].reshape(N_EXPERTS, d, D_EXPERT), w_up[0].reshape(N_EXPERTS, d, D_EXPERT),
                      w_down[0].reshape(N_EXPERTS, D_EXPERT, d), tm=EXPERT_ROWS)
    out = _combine(dst, y_rows.reshape(-1, ROW_TILE, LANES), h1.reshape(n, d), rw.reshape(n, LANES),
                   row(final_norm), tc=COMBINE_TOKENS)
    return out.reshape(b, seq, d)
```
